```python
import math
import jax, jax.numpy as jnp
from jax import lax
import numpy as np

D_MODEL = 1024
BATCH = 8
SEQ = 4096
DEPTH = 4

N_MIXERS = 2
RET_QK_DIM = 256
RET_HEADS = D_MODEL // RET_QK_DIM
RET_V_DIM = 2 * RET_QK_DIM
RET_QK = RET_HEADS * RET_QK_DIM
RET_V = RET_HEADS * RET_V_DIM
RET_CHUNK = 128
ROPE_BASE = 10000.0
DILATED_PAIRS = ((128, 1), (512, 4), (2048, 16))
ATT_GROUPS = 3
ATT_HEAD_DIM = 64
ATT_HEADS = D_MODEL // ATT_HEAD_DIM
ATT_GROUP_W = ATT_HEADS * ATT_HEAD_DIM
BAND_BLOCK = 128
NUM_BUCKETS = 32
MAX_DISTANCE = 2048
D_FF = ((8 * D_MODEL + 3 * 256 - 1) // (3 * 256)) * 256
DEEPNORM_ALPHA = (2 * DEPTH) ** 0.25
DEEPNORM_BETA = (8 * DEPTH) ** -0.25
LN_EPS = 1e-5
GN_EPS = 1e-5

kernel_name = "hybrid_retention_dilated_attn_deepnorm"


def _layer_norm(x, g, b):
    xf = x.astype(jnp.float32)
    mu = jnp.mean(xf, axis=-1, keepdims=True)
    var = jnp.mean(jnp.square(xf - mu), axis=-1, keepdims=True)
    y = (xf - mu) * lax.rsqrt(var + LN_EPS) * g.astype(jnp.float32) + b.astype(jnp.float32)
    return y.astype(x.dtype)


def _rope(x):
    B, T, H, d = x.shape
    inv = 1.0 / (ROPE_BASE ** jnp.linspace(0.0, 1.0, d // 2, dtype=jnp.float32))
    ang = jnp.arange(T, dtype=jnp.float32)[:, None] * inv[None, :]
    c = jnp.cos(ang)[None, :, None, :]
    s = jnp.sin(ang)[None, :, None, :]
    xp = x.reshape(B, T, H, d // 2, 2)
    a, b = xp[..., 0], xp[..., 1]
    return jnp.stack([a * c - b * s, a * s + b * c], axis=-1).reshape(B, T, H, d)


def retention(x, w_in, w_out):
    B, T, _ = x.shape
    H, dk, dv, C = RET_HEADS, RET_QK_DIM, RET_V_DIM, RET_CHUNK
    N = T // C
    proj = x @ w_in
    q, k, v, gate = jnp.split(proj, [RET_QK, 2 * RET_QK, 2 * RET_QK + RET_V], axis=-1)
    q = _rope(q.reshape(B, T, H, dk).astype(jnp.float32))
    k = _rope(k.reshape(B, T, H, dk).astype(jnp.float32)) * (dk ** -0.5)
    v = v.reshape(B, T, H, dv).astype(jnp.float32)

    def to_chunks(t):
        return t.reshape(B, N, C, H, t.shape[-1]).transpose(1, 0, 3, 2, 4)

    qc, kc, vc = to_chunks(q), to_chunks(k), to_chunks(v)
    log_gamma = jnp.log(1.0 - 2.0 ** (-5.0 - jnp.arange(H, dtype=jnp.float32)))
    idx = jnp.arange(C, dtype=jnp.float32)
    diff = idx[:, None] - idx[None, :]
    intra = jnp.where(diff[None] >= 0,
                      jnp.exp(jnp.maximum(diff, 0.0)[None] * log_gamma[:, None, None]), 0.0)
    xi = jnp.exp((idx[None, :] + 1.0) * log_gamma[:, None])[..., None]
    zeta = jnp.exp((C - 1.0 - idx[None, :]) * log_gamma[:, None])[..., None]
    gamma_c = jnp.exp(C * log_gamma)[:, None, None]

    def body(state, inp):
        qi, ki, vi = inp
        s = jnp.einsum('bhid,bhjd->bhij', qi, ki) * intra
        inner = jnp.einsum('bhij,bhje->bhie', s, vi)
        cross = jnp.einsum('bhid,bhde->bhie', qi * xi, state)
        state = gamma_c * state + jnp.einsum('bhjd,bhje->bhde', ki * zeta, vi)
        return state, inner + cross

    state0 = jnp.zeros((B, H, dk, dv), jnp.float32)
    _, y = lax.scan(body, state0, (qc, kc, vc))
    y = y.transpose(1, 0, 3, 2, 4).reshape(B, T, H, dv)
    mu = jnp.mean(y, axis=-1, keepdims=True)
    var = jnp.mean(jnp.square(y - mu), axis=-1, keepdims=True)
    y = ((y - mu) * lax.rsqrt(var + GN_EPS)).reshape(B, T, RET_V).astype(x.dtype)
    return (jax.nn.silu(gate) * y) @ w_out


def _t5_bucket(dist):
    max_exact = NUM_BUCKETS // 2
    d_f = jnp.maximum(dist, 1).astype(jnp.float32)
    large = max_exact + (jnp.log(d_f / max_exact) / math.log(MAX_DISTANCE / max_exact)
                         * (NUM_BUCKETS - max_exact)).astype(jnp.int32)
    large = jnp.minimum(large, NUM_BUCKETS - 1)
    return jnp.where(dist < max_exact, dist, large)


def _dilated_group(q, k, v, bias_table, window, dil):
    B, H, T, hd = q.shape
    blk = BAND_BLOCK
    steps = window // dil
    L = T // dil
    nb = -(-L // blk)
    Lp = nb * blk

    def to_blocks(t):
        t = t.reshape(B, H, L, dil, hd).transpose(0, 1, 3, 2, 4)
        t = jnp.pad(t, ((0, 0), (0, 0), (0, 0), (0, Lp - L), (0, 0)))
        return t.reshape(B, H, dil, nb, blk, hd)

    def band(t):
        prev = jnp.pad(t, ((0, 0), (0, 0), (0, 0), (1, 0), (0, 0), (0, 0)))[:, :, :, :-1]
        return jnp.concatenate([prev, t], axis=4)

    qb = to_blocks(q)
    kw = band(to_blocks(k))
    vw = band(to_blocks(v))
    qi = jnp.arange(blk)[:, None]
    kj = jnp.arange(2 * blk)[None, :]
    delta = qi + blk - kj
    band_ok = (delta >= 0) & (delta <= steps)
    key_ok = (jnp.arange(nb)[:, None, None] * blk - blk + kj[None]) >= 0
    mask = band_ok[None] & key_ok
    bucket = _t5_bucket(jnp.maximum(delta, 0) * dil)
    bias = bias_table[:, bucket].astype(jnp.float32)

    s = jnp.einsum('bhrnqd,bhrnkd->bhrnqk', qb, kw).astype(jnp.float32) * (hd ** -0.5)
    s = s + bias[None, :, None, None]
    s = jnp.where(mask[None, None, None], s, -jnp.inf)
    m = jnp.max(s, axis=-1, keepdims=True)
    e = jnp.exp(s - m)
    den = jnp.sum(e, axis=-1, keepdims=True)
    p = e / den
    lse = (m + jnp.log(den))[..., 0]
    o = jnp.einsum('bhrnqk,bhrnkd->bhrnqd', p.astype(v.dtype), vw)
    o = o.reshape(B, H, dil, Lp, hd)[:, :, :, :L].transpose(0, 1, 3, 2, 4).reshape(B, H, T, hd)
    lse = lse.reshape(B, H, dil, Lp)[..., :L].transpose(0, 1, 3, 2).reshape(B, H, T)
    return o, lse


def dilated_attention(x, w_in, w_out, rel_bias):
    B, T, _ = x.shape
    proj = (x @ w_in).reshape(B, T, 3, ATT_GROUPS, ATT_HEADS, ATT_HEAD_DIM)
    proj = proj.transpose(2, 3, 0, 4, 1, 5)
    outs, lses = [], []
    for g, (window, dil) in enumerate(DILATED_PAIRS):
        o, lse = _dilated_group(proj[0, g], proj[1, g], proj[2, g],
                                rel_bias[g * ATT_HEADS:(g + 1) * ATT_HEADS], window, dil)
        outs.append(o)
        lses.append(lse)
    w = jax.nn.softmax(jnp.stack(lses, axis=0), axis=0)
    o = jnp.sum(w[..., None] * jnp.stack(outs, axis=0).astype(jnp.float32), axis=0)
    o = o.transpose(0, 2, 1, 3).reshape(B, T, ATT_GROUP_W).astype(x.dtype)
    return o @ w_out


def swiglu_ffn(x, w_up, w_down):
    gate, up = jnp.split(x @ w_up, 2, axis=-1)
    return (jax.nn.silu(gate) * up) @ w_down


def setup_inputs(seed: int = 0) -> dict:
    key = jax.random.key(seed)
    ks = jax.random.split(key, 10)
    n_ret = (DEPTH + 1) // 2
    n_att = DEPTH // 2
    f32 = jnp.float32
    x = jax.random.normal(ks[0], (BATCH, SEQ, D_MODEL), f32)
    ret_w_in = jax.random.normal(ks[1], (n_ret, D_MODEL, 2 * RET_QK + 2 * RET_V), f32) * D_MODEL ** -0.5
    ret_w_out = jax.random.normal(ks[2], (n_ret, RET_V, D_MODEL), f32) * (RET_V ** -0.5 * DEEPNORM_BETA)
    attn_w_in = jax.random.normal(ks[3], (n_att, D_MODEL, 3 * ATT_GROUPS * ATT_GROUP_W), f32) * D_MODEL ** -0.5
    attn_w_out = jax.random.normal(ks[4], (n_att, ATT_GROUP_W, D_MODEL), f32) * (ATT_GROUP_W ** -0.5 * DEEPNORM_BETA)
    rel_bias = jax.random.normal(ks[5], (ATT_GROUPS * ATT_HEADS, NUM_BUCKETS), f32) * 0.2
    ffn_w_up = jax.random.normal(ks[6], (DEPTH, D_MODEL, 2 * D_FF), f32) * D_MODEL ** -0.5
    ffn_w_down = jax.random.normal(ks[7], (DEPTH, D_FF, D_MODEL), f32) * (D_FF ** -0.5 * DEEPNORM_BETA)
    ln_gain = 1.0 + 0.02 * jax.random.normal(ks[8], (DEPTH, 2, D_MODEL), f32)
    ln_bias = 0.02 * jax.random.normal(ks[9], (DEPTH, 2, D_MODEL), f32)
    return {"x": x, "ret_w_in": ret_w_in, "ret_w_out": ret_w_out,
            "attn_w_in": attn_w_in, "attn_w_out": attn_w_out, "rel_bias": rel_bias,
            "ffn_w_up": ffn_w_up, "ffn_w_down": ffn_w_down,
            "ln_gain": ln_gain, "ln_bias": ln_bias}


def reference(x, ret_w_in, ret_w_out, attn_w_in, attn_w_out, rel_bias,
              ffn_w_up, ffn_w_down, ln_gain, ln_bias):
    for i in range(DEPTH):
        j = i // N_MIXERS
        if i % N_MIXERS == 0:
            mix = retention(x, ret_w_in[j], ret_w_out[j])
        else:
            mix = dilated_attention(x, attn_w_in[j], attn_w_out[j], rel_bias)
        x = _layer_norm(DEEPNORM_ALPHA * x + mix, ln_gain[i, 0], ln_bias[i, 0])
        x = _layer_norm(DEEPNORM_ALPHA * x + swiglu_ffn(x, ffn_w_up[i], ffn_w_down[i]),
                        ln_gain[i, 1], ln_bias[i, 1])
    return x
```

```python
import functools
import math

import numpy as np
import jax
import jax.numpy as jnp
from jax import lax
from jax.experimental import pallas as pl
from jax.experimental.pallas import tpu as pltpu

F32 = jnp.float32
BF16 = jnp.bfloat16

D_MODEL = 1024
RET_QK_DIM = 256
RET_HEADS = D_MODEL // RET_QK_DIM
RET_V_DIM = 2 * RET_QK_DIM
RET_QK = RET_HEADS * RET_QK_DIM
RET_V = RET_HEADS * RET_V_DIM
ROPE_BASE = 10000.0
DILATED_PAIRS = ((128, 1), (512, 4), (2048, 16))
ATT_GROUPS = 3
ATT_HEAD_DIM = 64
ATT_HEADS = D_MODEL // ATT_HEAD_DIM
ATT_GROUP_W = ATT_HEADS * ATT_HEAD_DIM
BAND_BLOCK = 128
NUM_BUCKETS = 32
MAX_DISTANCE = 2048
D_FF = ((8 * D_MODEL + 3 * 256 - 1) // (3 * 256)) * 256
LN_EPS = 1e-5
GN_EPS = 1e-5

LANES = 128
VMEM_LIMIT_BYTES = 56 * 1024 * 1024
RET_CHUNK = 256
PROJ_TM = 1024
PROJ_TN = 1024
ATT_ROWS_MAX = 1024
TAIL_TM = 512
FF_CHUNK = 256
MASK_VALUE = -1e30


def _cparams(sem):
    return pltpu.CompilerParams(dimension_semantics=sem, vmem_limit_bytes=VMEM_LIMIT_BYTES)


def _resident(shape):
    nd = len(shape)
    return pl.BlockSpec(shape, lambda *_: (0,) * nd, pipeline_mode=pl.Buffered(1))


def _proj_kernel(*refs, mode, tn):
    if mode == "rope":
        x_ref, w_ref, cos_ref, sin_ref, o_ref = refs
    else:
        x_ref, w_ref, o_ref = refs
    j = pl.program_id(1)
    acc = jnp.dot(x_ref[...].astype(BF16), w_ref[...], preferred_element_type=F32)
    if mode == "rope":
        @pl.when(j < 2)
        def _():
            scale = jnp.where(j == 1, RET_QK_DIM ** -0.5, 1.0).astype(F32)
            c = cos_ref[...] * scale
            s = sin_ref[...] * scale
            half = RET_QK_DIM // 2
            for h in range(tn // RET_QK_DIM):
                lo = h * RET_QK_DIM
                a = acc[:, lo:lo + half]
                b = acc[:, lo + half:lo + 2 * half]
                o_ref[:, lo:lo + half] = (a * c - b * s).astype(BF16)
                o_ref[:, lo + half:lo + 2 * half] = (a * s + b * c).astype(BF16)

        @pl.when(j >= 2)
        def _():
            o_ref[...] = acc.astype(BF16)
    else:
        scale = jnp.where(j < ATT_GROUPS, ATT_HEAD_DIM ** -0.5, 1.0).astype(F32)
        o_ref[...] = (acc * scale).astype(BF16)


def _proj(x2d, w, mode, cos=None, sin=None, seq=None):
    m, k = x2d.shape
    n = w.shape[1]
    tm, tn = PROJ_TM, PROJ_TN
    assert m % tm == 0 and n % tn == 0
    in_specs = [pl.BlockSpec((tm, k), lambda i, j: (i, 0)),
                pl.BlockSpec((k, tn), lambda i, j: (0, j))]
    args = [x2d, w]
    if mode == "rope":
        assert tn == RET_QK and seq % tm == 0
        nseq = seq // tm
        spec = pl.BlockSpec((tm, RET_QK_DIM // 2), lambda i, j: (i % nseq, 0))
        in_specs += [spec, spec]
        args += [cos, sin]
    else:
        assert tn == ATT_GROUP_W
    return pl.pallas_call(
        functools.partial(_proj_kernel, mode=mode, tn=tn),
        grid=(m // tm, n // tn),
        in_specs=in_specs,
        out_specs=pl.BlockSpec((tm, tn), lambda i, j: (i, j)),
        out_shape=jax.ShapeDtypeStruct((m, n), BF16),
        compiler_params=_cparams(("parallel", "arbitrary")),
        name="proj_" + mode,
    )(*args)


def _ret_kernel(q_ref, k_ref, v_ref, g_ref, intra_ref, xi_ref, zeta_ref, o_ref, state_ref, *, gamma_c):
    @pl.when(pl.program_id(1) == 0)
    def _():
        state_ref[...] = jnp.zeros_like(state_ref)

    dk, dv = RET_QK_DIM, RET_V_DIM
    for h in range(RET_HEADS):
        q = q_ref[:, h * dk:(h + 1) * dk]
        k = k_ref[:, h * dk:(h + 1) * dk]
        v = v_ref[:, h * dv:(h + 1) * dv]
        s = lax.dot_general(q, k, (((1,), (1,)), ((), ())), preferred_element_type=F32)
        s = s * intra_ref[h]
        inner = jnp.dot(s.astype(BF16), v, preferred_element_type=F32)
        st = state_ref[h]
        cross = jnp.dot(q, st.astype(BF16), preferred_element_type=F32) * xi_ref[h]
        kz = (k.astype(F32) * zeta_ref[h]).astype(BF16)
        upd = lax.dot_general(kz, v, (((0,), (0,)), ((), ())), preferred_element_type=F32)
        state_ref[h] = gamma_c[h] * st + upd
        y = inner + cross
        mu = jnp.mean(y, axis=-1, keepdims=True)
        yc = y - mu
        var = jnp.mean(yc * yc, axis=-1, keepdims=True)
        yn = yc * lax.rsqrt(var + GN_EPS)
        g = g_ref[:, h * dv:(h + 1) * dv].astype(F32)
        silu = g * (1.0 / (1.0 + jnp.exp(-g)))
        o_ref[:, h * dv:(h + 1) * dv] = (silu * yn).astype(BF16)


def _ret_tables(c):
    h = np.arange(RET_HEADS, dtype=np.float64)
    log_gamma = np.log(1.0 - 2.0 ** (-5.0 - h))
    idx = np.arange(c, dtype=np.float64)
    diff = idx[:, None] - idx[None, :]
    intra = np.where(diff[None] >= 0, np.exp(np.maximum(diff, 0.0)[None] * log_gamma[:, None, None]), 0.0)
    xi = np.exp((idx[None, :] + 1.0) * log_gamma[:, None])[..., None]
    zeta = np.exp((c - 1.0 - idx[None, :]) * log_gamma[:, None])[..., None]
    gamma_c = tuple(float(np.float32(v)) for v in np.exp(c * log_gamma))
    return (jnp.asarray(intra, F32), jnp.asarray(xi, F32), jnp.asarray(zeta, F32), gamma_c)


def _retention_core(p, batch, seq):
    c = RET_CHUNK
    assert seq % c == 0
    nc = seq // c
    intra, xi, zeta, gamma_c = _ret_tables(c)
    row = lambda b, i: b * nc + i
    return pl.pallas_call(
        functools.partial(_ret_kernel, gamma_c=gamma_c),
        grid=(batch, nc),
        in_specs=[pl.BlockSpec((c, RET_QK), lambda b, i: (row(b, i), 0)),
                  pl.BlockSpec((c, RET_QK), lambda b, i: (row(b, i), 1)),
                  pl.BlockSpec((c, RET_V), lambda b, i: (row(b, i), 1)),
                  pl.BlockSpec((c, RET_V), lambda b, i: (row(b, i), 2)),
                  _resident(intra.shape), _resident(xi.shape), _resident(zeta.shape)],
        out_specs=pl.BlockSpec((c, RET_V), lambda b, i: (row(b, i), 0)),
        out_shape=jax.ShapeDtypeStruct((batch * seq, RET_V), BF16),
        scratch_shapes=[pltpu.VMEM((RET_HEADS, RET_QK_DIM, RET_V_DIM), F32)],
        compiler_params=_cparams(("parallel", "arbitrary")),
        name="retention_core",
    )(p, p, p, p, intra, xi, zeta)


def _t5_bucket(dist):
    max_exact = NUM_BUCKETS // 2
    d_f = jnp.maximum(dist, 1).astype(jnp.float32)
    large = max_exact + (jnp.log(d_f / max_exact) / math.log(MAX_DISTANCE / max_exact)
                         * (NUM_BUCKETS - max_exact)).astype(jnp.int32)
    large = jnp.minimum(large, NUM_BUCKETS - 1)
    return jnp.where(dist < max_exact, dist, large)


def _band_buckets():
    blk = BAND_BLOCK
    qi = jnp.arange(blk)[:, None]
    kj = jnp.arange(2 * blk)[None, :]
    delta = qi + blk - kj
    out = []
    for window, dil in DILATED_PAIRS:
        steps = window // dil
        ok = (delta >= 0) & (delta <= steps)
        out.append(jnp.where(ok, _t5_bucket(jnp.maximum(delta, 0) * dil), -1))
    return jnp.stack(out).astype(jnp.int32)


def _bias_kernel(tab_ref, bkt_ref, o_ref):
    gh = pl.program_id(0)
    bkt = bkt_ref[0]
    acc = jnp.full(bkt.shape, MASK_VALUE, F32)
    for b in range(NUM_BUCKETS):
        acc = jnp.where(bkt == b, tab_ref[gh, b], acc)
    o_ref[0] = acc


def _band_bias(rel_bias):
    gh = rel_bias.shape[0]
    blk = BAND_BLOCK
    return pl.pallas_call(
        _bias_kernel,
        grid=(gh,),
        in_specs=[pl.BlockSpec(memory_space=pltpu.SMEM),
                  pl.BlockSpec((1, blk, 2 * blk), lambda i: (i // ATT_HEADS, 0, 0))],
        out_specs=pl.BlockSpec((1, blk, 2 * blk), lambda i: (i, 0, 0)),
        out_shape=jax.ShapeDtypeStruct((gh, blk, 2 * blk), F32),
        compiler_params=_cparams(("parallel",)),
        name="band_bias",
    )(rel_bias, _band_buckets())


def _attn_kernel(q_ref, k_ref, v_ref, kh_ref, vh_ref, bias_ref, o_ref, lse_ref, *, nblk):
    blk, hd = BAND_BLOCK, ATT_HEAD_DIM
    first_prog = pl.program_id(2) == 0
    lane = lax.broadcasted_iota(jnp.int32, (blk, LANES), 1)
    col = lax.broadcasted_iota(jnp.int32, (blk, 2 * blk), 1)

    def block(n, carry):
        row = pl.multiple_of(n * blk, blk)
        prow = pl.multiple_of(jnp.maximum(n - 1, 0) * blk, blk)
        first_blk = n == 0
        no_prev = jnp.logical_and(first_blk, first_prog)
        prev_dead = jnp.logical_and(no_prev, col < blk)
        lse_tile = jnp.zeros((blk, LANES), F32)
        for hp in range(ATT_HEADS // 2):
            outs = []
            for hh in range(2):
                h = 2 * hp + hh
                sl = slice(h * hd, (h + 1) * hd)
                q = q_ref[0, pl.ds(row, blk), sl]
                kc = k_ref[0, pl.ds(row, blk), sl]
                vc = v_ref[0, pl.ds(row, blk), sl]
                kp = jnp.where(first_blk, kh_ref[0, :, sl], k_ref[0, pl.ds(prow, blk), sl])
                vp = jnp.where(first_blk, vh_ref[0, :, sl], v_ref[0, pl.ds(prow, blk), sl])
                kb = jnp.concatenate([kp, kc], axis=0)
                vb = jnp.concatenate([vp, vc], axis=0)
                s = lax.dot_general(q, kb, (((1,), (1,)), ((), ())), preferred_element_type=F32)
                s = s + bias_ref[h]
                s = jnp.where(prev_dead, MASK_VALUE, s)
                m = jnp.max(s, axis=-1, keepdims=True)
                e = jnp.exp(s - m)
                den = jnp.sum(e, axis=-1, keepdims=True)
                o = jnp.dot(e.astype(BF16), vb, preferred_element_type=F32) * (1.0 / den)
                lse_tile = jnp.where(lane == h, m + jnp.log(den), lse_tile)
                outs.append(o)
            o_ref[0, pl.ds(row, blk), hp * 2 * hd:(hp + 1) * 2 * hd] = (
                jnp.concatenate(outs, axis=1).astype(BF16))
        lse_ref[0, pl.ds(row, blk), :] = lse_tile
        return carry

    lax.fori_loop(0, nblk, block, 0)


def _attn_group(p, bias, g, batch, seq):
    window, dil = DILATED_PAIRS[g]
    assert window // dil == BAND_BLOCK
    blk, w = BAND_BLOCK, ATT_GROUP_W
    length = seq // dil
    rows = min(length, ATT_ROWS_MAX)
    assert length % rows == 0 and rows % blk == 0
    nprog, nblk = length // rows, rows // blk
    ncol = 3 * ATT_GROUPS
    pv = p.reshape(batch, length, dil * ncol * w)
    main = lambda part: pl.BlockSpec(
        (1, rows, w), lambda b, r, s: (b, s, r * ncol + part * ATT_GROUPS + g))
    halo = lambda part: pl.BlockSpec(
        (1, blk, w), lambda b, r, s: (b, jnp.maximum(s * nblk - 1, 0), r * ncol + part * ATT_GROUPS + g))
    o, lse = pl.pallas_call(
        functools.partial(_attn_kernel, nblk=nblk),
        grid=(batch, dil, nprog),
        in_specs=[main(0), main(1), main(2), halo(1), halo(2),
                  pl.BlockSpec((ATT_HEADS, blk, 2 * blk), lambda b, r, s: (g, 0, 0))],
        out_specs=[pl.BlockSpec((1, rows, w), lambda b, r, s: (b, s, r)),
                   pl.BlockSpec((1, rows, LANES), lambda b, r, s: (b, s, r))],
        out_shape=[jax.ShapeDtypeStruct((batch, length, dil * w), BF16),
                   jax.ShapeDtypeStruct((batch, length, dil * LANES), F32)],
        compiler_params=_cparams(("parallel", "parallel", "arbitrary")),
        name=f"attn_g{g}",
    )(pv, pv, pv, pv, pv, bias)
    return o.reshape(batch * seq, w), lse.reshape(batch * seq, LANES)


def _layer_norm(v, gain, bias):
    mu = jnp.mean(v, axis=-1, keepdims=True)
    vc = v - mu
    var = jnp.mean(vc * vc, axis=-1, keepdims=True)
    return vc * lax.rsqrt(var + LN_EPS) * gain + bias


def _tail_kernel(*refs, merge, alpha):
    if merge:
        (o0_ref, o1_ref, o2_ref, l0_ref, l1_ref, l2_ref, ex_ref,
         x_ref, wo_ref, wu_ref, wd_ref, ln_ref, out_ref) = refs
        l0, l1, l2 = l0_ref[...], l1_ref[...], l2_ref[...]
        mx = jnp.maximum(jnp.maximum(l0, l1), l2)
        e0, e1, e2 = jnp.exp(l0 - mx), jnp.exp(l1 - mx), jnp.exp(l2 - mx)
        inv = 1.0 / (e0 + e1 + e2)
        z = None
        for e, o_ref in ((e0, o0_ref), (e1, o1_ref), (e2, o2_ref)):
            wgt = e * inv
            hi = wgt.astype(BF16)
            lo = (wgt - hi.astype(F32)).astype(BF16)
            wexp = (jnp.dot(hi, ex_ref[...], preferred_element_type=F32)
                    + jnp.dot(lo, ex_ref[...], preferred_element_type=F32))
            term = wexp * o_ref[...].astype(F32)
            z = term if z is None else z + term
        z = z.astype(BF16)
    else:
        z_ref, x_ref, wo_ref, wu_ref, wd_ref, ln_ref, out_ref = refs
        z = z_ref[...]
    mix = jnp.dot(z, wo_ref[...], preferred_element_type=F32)
    x1 = _layer_norm(alpha * x_ref[...] + mix, ln_ref[0:1, :], ln_ref[1:2, :])
    x1b = x1.astype(BF16)
    acc = jnp.zeros(x1.shape, F32)
    for c0 in range(0, D_FF, FF_CHUNK):
        gate = jnp.dot(x1b, wu_ref[:, c0:c0 + FF_CHUNK], preferred_element_type=F32)
        up = jnp.dot(x1b, wu_ref[:, D_FF + c0:D_FF + c0 + FF_CHUNK], preferred_element_type=F32)
        hid = gate * (1.0 / (1.0 + jnp.exp(-gate))) * up
        acc = acc + jnp.dot(hid.astype(BF16), wd_ref[c0:c0 + FF_CHUNK, :], preferred_element_type=F32)
    out_ref[...] = _layer_norm(alpha * x1 + acc, ln_ref[2:3, :], ln_ref[3:4, :])


def _tail(mix_in, x2d, w_out, w_up, w_down, ln, alpha):
    m, d = x2d.shape
    tm = TAIL_TM
    assert m % tm == 0 and D_FF % FF_CHUNK == 0
    merge = isinstance(mix_in, tuple)
    row = lambda width: pl.BlockSpec((tm, width), lambda i: (i, 0))
    if merge:
        expand = np.zeros((LANES, ATT_GROUP_W), np.float32)
        for h in range(ATT_HEADS):
            expand[h, h * ATT_HEAD_DIM:(h + 1) * ATT_HEAD_DIM] = 1.0
        expand = jnp.asarray(expand, BF16)
        args = list(mix_in) + [expand]
        in_specs = [row(ATT_GROUP_W)] * 3 + [row(LANES)] * 3 + [_resident(expand.shape)]
    else:
        args = [mix_in]
        in_specs = [row(mix_in.shape[1])]
    args += [x2d, w_out, w_up, w_down, ln]
    in_specs += [row(d), _resident(w_out.shape), _resident(w_up.shape), _resident(w_down.shape),
                 _resident(ln.shape)]
    return pl.pallas_call(
        functools.partial(_tail_kernel, merge=merge, alpha=alpha),
        grid=(m // tm,),
        in_specs=in_specs,
        out_specs=row(d),
        out_shape=jax.ShapeDtypeStruct((m, d), F32),
        compiler_params=_cparams(("parallel",)),
        name="tail_attn" if merge else "tail_ret",
    )(*args)


def _rope_tables(seq):
    d = RET_QK_DIM
    inv = 1.0 / (ROPE_BASE ** jnp.linspace(0.0, 1.0, d // 2, dtype=jnp.float32))
    ang = jnp.arange(seq, dtype=jnp.float32)[:, None] * inv[None, :]
    return jnp.cos(ang), jnp.sin(ang)


def _ret_w_in_layout(w_in):
    perm = np.arange(w_in.shape[1])
    half = RET_QK_DIM // 2
    for h in range(2 * RET_HEADS):
        base = h * RET_QK_DIM
        perm[base:base + half] = base + 2 * np.arange(half)
        perm[base + half:base + RET_QK_DIM] = base + 2 * np.arange(half) + 1
    return w_in[:, perm].astype(BF16)


def kernel(x, ret_w_in, ret_w_out, attn_w_in, attn_w_out, rel_bias, ffn_w_up, ffn_w_down, ln_gain, ln_bias):
    batch, seq, d = x.shape
    depth = ffn_w_up.shape[0]
    alpha = (2 * depth) ** 0.25
    cos, sin = _rope_tables(seq)
    bias = _band_bias(rel_bias)
    h = x.reshape(batch * seq, d)
    for i in range(depth):
        j = i // 2
        ln = jnp.stack([ln_gain[i, 0], ln_bias[i, 0], ln_gain[i, 1], ln_bias[i, 1]])
        if i % 2 == 0:
            p = _proj(h, _ret_w_in_layout(ret_w_in[j]), "rope", cos, sin, seq)
            mix_in = _retention_core(p, batch, seq)
            w_out = ret_w_out[j]
        else:
            p = _proj(h, attn_w_in[j].astype(BF16), "attn")
            outs = [_attn_group(p, bias, g, batch, seq) for g in range(ATT_GROUPS)]
            mix_in = tuple(o for o, _ in outs) + tuple(l for _, l in outs)
            w_out = attn_w_out[j]
        h = _tail(mix_in, h, w_out.astype(BF16), ffn_w_up[i].astype(BF16),
                  ffn_w_down[i].astype(BF16), ln, alpha)
    return h.reshape(batch, seq, d)
```

```python
import functools
import math

import numpy as np
import jax
import jax.numpy as jnp
from jax import lax
from jax.experimental import pallas as pl
from jax.experimental.pallas import tpu as pltpu

F32 = jnp.float32
BF16 = jnp.bfloat16

D_MODEL = 1024
RET_QK_DIM = 256
RET_HEADS = D_MODEL // RET_QK_DIM
RET_V_DIM = 2 * RET_QK_DIM
RET_QK = RET_HEADS * RET_QK_DIM
RET_V = RET_HEADS * RET_V_DIM
ROPE_BASE = 10000.0
DILATED_PAIRS = ((128, 1), (512, 4), (2048, 16))
ATT_GROUPS = 3
ATT_HEAD_DIM = 64
ATT_HEADS = D_MODEL // ATT_HEAD_DIM
ATT_GROUP_W = ATT_HEADS * ATT_HEAD_DIM
BAND_BLOCK = 128
NUM_BUCKETS = 32
MAX_DISTANCE = 2048
D_FF = ((8 * D_MODEL + 3 * 256 - 1) // (3 * 256)) * 256
LN_EPS = 1e-5
GN_EPS = 1e-5

LANES = 128
VMEM_LIMIT_BYTES = 56 * 1024 * 1024
RET_CHUNK = 256
PROJ_TM = 1024
PROJ_TN = 1024
ATT_ROWS_MAX = 1024
TAIL_TM = 512
FF_CHUNK = 256
MASK_VALUE = -1e30


def _cparams(sem):
    return pltpu.CompilerParams(dimension_semantics=sem, vmem_limit_bytes=VMEM_LIMIT_BYTES)


def _resident(shape):
    nd = len(shape)
    return pl.BlockSpec(shape, lambda *_: (0,) * nd, pipeline_mode=pl.Buffered(1))


def _proj_rope_kernel(x_ref, w_ref, cos_ref, sin_ref, o_ref, *, tn):
    j = pl.program_id(1)
    acc = jnp.dot(x_ref[...].astype(BF16), w_ref[...], preferred_element_type=F32)

    @pl.when(j < 2)
    def _():
        scale = jnp.where(j == 1, RET_QK_DIM ** -0.5, 1.0).astype(F32)
        c = cos_ref[...] * scale
        s = sin_ref[...] * scale
        half = RET_QK_DIM // 2
        for h in range(tn // RET_QK_DIM):
            lo = h * RET_QK_DIM
            a = acc[:, lo:lo + half]
            b = acc[:, lo + half:lo + 2 * half]
            o_ref[:, lo:lo + half] = (a * c - b * s).astype(BF16)
            o_ref[:, lo + half:lo + 2 * half] = (a * s + b * c).astype(BF16)

    @pl.when(j >= 2)
    def _():
        o_ref[...] = acc.astype(BF16)


def _proj_rope(x2d, w, cos, sin, seq):
    m, k = x2d.shape
    n = w.shape[1]
    tm, tn = PROJ_TM, PROJ_TN
    assert m % tm == 0 and n % tn == 0 and tn == RET_QK and seq % tm == 0
    nseq = seq // tm
    table = pl.BlockSpec((tm, RET_QK_DIM // 2), lambda i, j: (i % nseq, 0))
    return pl.pallas_call(
        functools.partial(_proj_rope_kernel, tn=tn),
        grid=(m // tm, n // tn),
        in_specs=[pl.BlockSpec((tm, k), lambda i, j: (i, 0)),
                  pl.BlockSpec((k, tn), lambda i, j: (0, j)), table, table],
        out_specs=pl.BlockSpec((tm, tn), lambda i, j: (i, j)),
        out_shape=jax.ShapeDtypeStruct((m, n), BF16),
        compiler_params=_cparams(("parallel", "arbitrary")),
        name="proj_rope",
    )(x2d, w, cos, sin)


def _proj_attn_kernel(x_ref, w_ref, o_ref, slab_ref, xb_ref, *, dil):
    j = pl.program_id(1)
    tm, k = x_ref.shape
    rows = tm // dil

    @pl.when(j == 0)
    def _():
        if dil == 1:
            xb_ref[...] = x_ref[...].astype(BF16)
        else:
            for s in range(k // LANES):
                slab_ref[s] = x_ref[:, s * LANES:(s + 1) * LANES]
            for r in range(dil):
                for s in range(k // LANES):
                    xb_ref[r * rows:(r + 1) * rows, s * LANES:(s + 1) * LANES] = (
                        slab_ref[s, pl.ds(r, rows, stride=dil), :].astype(BF16))

    acc = jnp.dot(xb_ref[...], w_ref[...], preferred_element_type=F32)
    scale = jnp.where(j == 0, ATT_HEAD_DIM ** -0.5, 1.0).astype(F32)
    res = (acc * scale).astype(BF16)
    for r in range(dil):
        o_ref[0, r] = res[r * rows:(r + 1) * rows, :]


def _proj_attn(x2d, w, dil, batch, seq):
    m, k = x2d.shape
    n = w.shape[1]
    tm, tn = PROJ_TM, PROJ_TN
    assert seq % tm == 0 and n % tn == 0 and tn == ATT_GROUP_W and tm % (dil * 16) == 0
    nseq = seq // tm
    return pl.pallas_call(
        functools.partial(_proj_attn_kernel, dil=dil),
        grid=(m // tm, n // tn),
        in_specs=[pl.BlockSpec((tm, k), lambda i, j: (i, 0)),
                  pl.BlockSpec((k, tn), lambda i, j: (0, j))],
        out_specs=pl.BlockSpec((1, dil, tm // dil, tn), lambda i, j: (i // nseq, 0, i % nseq, j)),
        out_shape=jax.ShapeDtypeStruct((batch, dil, seq // dil, n), BF16),
        scratch_shapes=[pltpu.VMEM((k // LANES, tm, LANES), F32), pltpu.VMEM((tm, k), BF16)],
        compiler_params=_cparams(("parallel", "arbitrary")),
        name=f"proj_attn_d{dil}",
    )(x2d, w)


def _ret_kernel(q_ref, k_ref, v_ref, g_ref, intra_ref, xi_ref, zeta_ref, o_ref, state_ref, *, gamma_c):
    @pl.when(pl.program_id(1) == 0)
    def _():
        state_ref[...] = jnp.zeros_like(state_ref)

    dk, dv = RET_QK_DIM, RET_V_DIM
    for h in range(RET_HEADS):
        q = q_ref[:, h * dk:(h + 1) * dk]
        k = k_ref[:, h * dk:(h + 1) * dk]
        v = v_ref[:, h * dv:(h + 1) * dv]
        s = lax.dot_general(q, k, (((1,), (1,)), ((), ())), preferred_element_type=F32)
        s = s * intra_ref[h]
        inner = jnp.dot(s.astype(BF16), v, preferred_element_type=F32)
        st = state_ref[h]
        cross = jnp.dot(q, st.astype(BF16), preferred_element_type=F32) * xi_ref[h]
        kz = (k.astype(F32) * zeta_ref[h]).astype(BF16)
        upd = lax.dot_general(kz, v, (((0,), (0,)), ((), ())), preferred_element_type=F32)
        state_ref[h] = gamma_c[h] * st + upd
        y = inner + cross
        mu = jnp.mean(y, axis=-1, keepdims=True)
        yc = y - mu
        var = jnp.mean(yc * yc, axis=-1, keepdims=True)
        yn = yc * lax.rsqrt(var + GN_EPS)
        g = g_ref[:, h * dv:(h + 1) * dv].astype(F32)
        silu = g * (1.0 / (1.0 + jnp.exp(-g)))
        o_ref[:, h * dv:(h + 1) * dv] = (silu * yn).astype(BF16)


def _ret_tables(c):
    h = np.arange(RET_HEADS, dtype=np.float64)
    log_gamma = np.log(1.0 - 2.0 ** (-5.0 - h))
    idx = np.arange(c, dtype=np.float64)
    diff = idx[:, None] - idx[None, :]
    intra = np.where(diff[None] >= 0, np.exp(np.maximum(diff, 0.0)[None] * log_gamma[:, None, None]), 0.0)
    xi = np.exp((idx[None, :] + 1.0) * log_gamma[:, None])[..., None]
    zeta = np.exp((c - 1.0 - idx[None, :]) * log_gamma[:, None])[..., None]
    gamma_c = tuple(float(np.float32(v)) for v in np.exp(c * log_gamma))
    return (jnp.asarray(intra, F32), jnp.asarray(xi, F32), jnp.asarray(zeta, F32), gamma_c)


def _retention_core(p, batch, seq):
    c = RET_CHUNK
    assert seq % c == 0
    nc = seq // c
    intra, xi, zeta, gamma_c = _ret_tables(c)
    row = lambda b, i: b * nc + i
    return pl.pallas_call(
        functools.partial(_ret_kernel, gamma_c=gamma_c),
        grid=(batch, nc),
        in_specs=[pl.BlockSpec((c, RET_QK), lambda b, i: (row(b, i), 0)),
                  pl.BlockSpec((c, RET_QK), lambda b, i: (row(b, i), 1)),
                  pl.BlockSpec((c, RET_V), lambda b, i: (row(b, i), 1)),
                  pl.BlockSpec((c, RET_V), lambda b, i: (row(b, i), 2)),
                  _resident(intra.shape), _resident(xi.shape), _resident(zeta.shape)],
        out_specs=pl.BlockSpec((c, RET_V), lambda b, i: (row(b, i), 0)),
        out_shape=jax.ShapeDtypeStruct((batch * seq, RET_V), BF16),
        scratch_shapes=[pltpu.VMEM((RET_HEADS, RET_QK_DIM, RET_V_DIM), F32)],
        compiler_params=_cparams(("parallel", "arbitrary")),
        name="retention_core",
    )(p, p, p, p, intra, xi, zeta)


def _t5_bucket(dist):
    max_exact = NUM_BUCKETS // 2
    d_f = jnp.maximum(dist, 1).astype(jnp.float32)
    large = max_exact + (jnp.log(d_f / max_exact) / math.log(MAX_DISTANCE / max_exact)
                         * (NUM_BUCKETS - max_exact)).astype(jnp.int32)
    large = jnp.minimum(large, NUM_BUCKETS - 1)
    return jnp.where(dist < max_exact, dist, large)


def _band_buckets():
    blk = BAND_BLOCK
    qi = jnp.arange(blk)[:, None]
    kj = jnp.arange(2 * blk)[None, :]
    delta = qi + blk - kj
    out = []
    for window, dil in DILATED_PAIRS:
        steps = window // dil
        ok = (delta >= 0) & (delta <= steps)
        out.append(jnp.where(ok, _t5_bucket(jnp.maximum(delta, 0) * dil), -1))
    return jnp.stack(out).astype(jnp.int32)


def _bias_kernel(tab_ref, bkt_ref, o_ref):
    gh = pl.program_id(0)
    bkt = bkt_ref[0]
    acc = jnp.full(bkt.shape, MASK_VALUE, F32)
    for b in range(NUM_BUCKETS):
        acc = jnp.where(bkt == b, tab_ref[gh, b], acc)
    o_ref[0] = acc


def _band_bias(rel_bias):
    gh = rel_bias.shape[0]
    blk = BAND_BLOCK
    return pl.pallas_call(
        _bias_kernel,
        grid=(gh,),
        in_specs=[pl.BlockSpec(memory_space=pltpu.SMEM),
                  pl.BlockSpec((1, blk, 2 * blk), lambda i: (i // ATT_HEADS, 0, 0))],
        out_specs=pl.BlockSpec((1, blk, 2 * blk), lambda i: (i, 0, 0)),
        out_shape=jax.ShapeDtypeStruct((gh, blk, 2 * blk), F32),
        compiler_params=_cparams(("parallel",)),
        name="band_bias",
    )(rel_bias, _band_buckets())


def _attn_kernel(q_ref, k_ref, v_ref, kh_ref, vh_ref, bias_ref, o_ref, lse_ref, *, nblk):
    blk, hd = BAND_BLOCK, ATT_HEAD_DIM
    first_prog = pl.program_id(2) == 0
    lane = lax.broadcasted_iota(jnp.int32, (blk, LANES), 1)
    col = lax.broadcasted_iota(jnp.int32, (blk, 2 * blk), 1)

    def block(n, carry):
        row = pl.multiple_of(n * blk, blk)
        prow = pl.multiple_of(jnp.maximum(n - 1, 0) * blk, blk)
        first_blk = n == 0
        no_prev = jnp.logical_and(first_blk, first_prog)
        prev_dead = jnp.logical_and(no_prev, col < blk)
        lse_tile = jnp.zeros((blk, LANES), F32)
        for hp in range(ATT_HEADS // 2):
            outs = []
            for hh in range(2):
                h = 2 * hp + hh
                sl = slice(h * hd, (h + 1) * hd)
                q = q_ref[0, pl.ds(row, blk), sl]
                kc = k_ref[0, pl.ds(row, blk), sl]
                vc = v_ref[0, pl.ds(row, blk), sl]
                kp = jnp.where(first_blk, kh_ref[0, :, sl], k_ref[0, pl.ds(prow, blk), sl])
                vp = jnp.where(first_blk, vh_ref[0, :, sl], v_ref[0, pl.ds(prow, blk), sl])
                kb = jnp.concatenate([kp, kc], axis=0)
                vb = jnp.concatenate([vp, vc], axis=0)
                s = lax.dot_general(q, kb, (((1,), (1,)), ((), ())), preferred_element_type=F32)
                s = s + bias_ref[h]
                s = jnp.where(prev_dead, MASK_VALUE, s)
                m = jnp.max(s, axis=-1, keepdims=True)
                e = jnp.exp(s - m)
                den = jnp.sum(e, axis=-1, keepdims=True)
                o = jnp.dot(e.astype(BF16), vb, preferred_element_type=F32) * (1.0 / den)
                lse_tile = jnp.where(lane == h, m + jnp.log(den), lse_tile)
                outs.append(o)
            o_ref[0, pl.ds(row, blk), hp * 2 * hd:(hp + 1) * 2 * hd] = (
                jnp.concatenate(outs, axis=1).astype(BF16))
        lse_ref[0, pl.ds(row, blk), :] = lse_tile
        return carry

    lax.fori_loop(0, nblk, block, 0)


def _attn_group(p, bias, g):
    window, dil = DILATED_PAIRS[g]
    assert window // dil == BAND_BLOCK
    blk, w = BAND_BLOCK, ATT_GROUP_W
    batch, _, length, _ = p.shape
    rows = min(length, ATT_ROWS_MAX)
    assert length % rows == 0 and rows % blk == 0
    nprog, nblk = length // rows, rows // blk
    main = lambda part: pl.BlockSpec((None, 1, rows, w), lambda b, r, s: (b, r, s, part))
    halo = lambda part: pl.BlockSpec(
        (None, 1, blk, w), lambda b, r, s: (b, r, jnp.maximum(s * nblk - 1, 0), part))
    return pl.pallas_call(
        functools.partial(_attn_kernel, nblk=nblk),
        grid=(batch, dil, nprog),
        in_specs=[main(0), main(1), main(2), halo(1), halo(2),
                  pl.BlockSpec((ATT_HEADS, blk, 2 * blk), lambda b, r, s: (g, 0, 0))],
        out_specs=[pl.BlockSpec((None, 1, rows, w), lambda b, r, s: (b, r, s, 0)),
                   pl.BlockSpec((None, 1, rows, LANES), lambda b, r, s: (b, r, s, 0))],
        out_shape=[jax.ShapeDtypeStruct((batch, dil, length, w), BF16),
                   jax.ShapeDtypeStruct((batch, dil, length, LANES), F32)],
        compiler_params=_cparams(("parallel", "parallel", "arbitrary")),
        name=f"attn_g{g}",
    )(p, p, p, p, p, bias)


def _layer_norm(v, gain, bias):
    mu = jnp.mean(v, axis=-1, keepdims=True)
    vc = v - mu
    var = jnp.mean(vc * vc, axis=-1, keepdims=True)
    return vc * lax.rsqrt(var + LN_EPS) * gain + bias


def _token_order(src_ref, dst_ref):
    dil, rows, width = src_ref.shape
    for r in range(dil):
        for s in range(width // LANES):
            dst_ref[s, pl.ds(r, rows, stride=dil), :] = (
                src_ref[r, :, s * LANES:(s + 1) * LANES].astype(F32))


def _tail_kernel(*refs, merge, alpha):
    if merge:
        (o0_ref, o1_ref, o2_ref, l0_ref, l1_ref, l2_ref, ex_ref, x_ref, wo_ref, wu_ref, wd_ref,
         ln_ref, out_ref, *scratch) = refs
        o_refs, l_refs = (o0_ref, o1_ref, o2_ref), (l0_ref, l1_ref, l2_ref)
        outs, lses, k = [], [], 0
        for o_ref, l_ref in zip(o_refs, l_refs):
            if o_ref.shape[0] == 1:
                outs.append(o_ref[0].astype(F32))
                lses.append(l_ref[0])
            else:
                so_ref, sl_ref = scratch[k], scratch[k + 1]
                k += 2
                _token_order(o_ref, so_ref)
                _token_order(l_ref, sl_ref)
                outs.append(jnp.concatenate([so_ref[s] for s in range(so_ref.shape[0])], axis=1))
                lses.append(sl_ref[0])
        mx = jnp.maximum(jnp.maximum(lses[0], lses[1]), lses[2])
        es = [jnp.exp(l - mx) for l in lses]
        inv = 1.0 / (es[0] + es[1] + es[2])
        z = None
        for e, o in zip(es, outs):
            wgt = (e * inv).astype(BF16)
            term = jnp.dot(wgt, ex_ref[...], preferred_element_type=F32) * o
            z = term if z is None else z + term
        z = z.astype(BF16)
    else:
        z_ref, x_ref, wo_ref, wu_ref, wd_ref, ln_ref, out_ref = refs
        z = z_ref[...]
    mix = jnp.dot(z, wo_ref[...], preferred_element_type=F32)
    x1 = _layer_norm(alpha * x_ref[...] + mix, ln_ref[0:1, :], ln_ref[1:2, :])
    x1b = x1.astype(BF16)
    acc = jnp.zeros(x1.shape, F32)
    for c0 in range(0, D_FF, FF_CHUNK):
        gate = jnp.dot(x1b, wu_ref[:, c0:c0 + FF_CHUNK], preferred_element_type=F32)
        up = jnp.dot(x1b, wu_ref[:, D_FF + c0:D_FF + c0 + FF_CHUNK], preferred_element_type=F32)
        hid = gate * (1.0 / (1.0 + jnp.exp(-gate))) * up
        acc = acc + jnp.dot(hid.astype(BF16), wd_ref[c0:c0 + FF_CHUNK, :], preferred_element_type=F32)
    out_ref[...] = _layer_norm(alpha * x1 + acc, ln_ref[2:3, :], ln_ref[3:4, :])


def _tail(mix_in, x2d, w_out, w_up, w_down, ln, alpha):
    m, d = x2d.shape
    tm = TAIL_TM
    assert m % tm == 0 and D_FF % FF_CHUNK == 0
    merge = isinstance(mix_in, tuple)
    row = lambda width: pl.BlockSpec((tm, width), lambda i: (i, 0))
    scratch = []
    if merge:
        expand = np.zeros((LANES, ATT_GROUP_W), np.float32)
        for h in range(ATT_HEADS):
            expand[h, h * ATT_HEAD_DIM:(h + 1) * ATT_HEAD_DIM] = 1.0
        expand = jnp.asarray(expand, BF16)
        in_specs = []
        for a in mix_in:
            batch, dil, length, width = a.shape
            nseq = dil * length // tm
            assert tm % (dil * 16) == 0
            in_specs.append(pl.BlockSpec((None, dil, tm // dil, width),
                                         lambda i, nseq=nseq: (i // nseq, 0, i % nseq, 0)))
        for a in mix_in[:ATT_GROUPS]:
            if a.shape[1] > 1:
                scratch += [pltpu.VMEM((ATT_GROUP_W // LANES, tm, LANES), F32),
                            pltpu.VMEM((1, tm, LANES), F32)]
        args = list(mix_in) + [expand]
        in_specs.append(_resident(expand.shape))
    else:
        args = [mix_in]
        in_specs = [row(mix_in.shape[1])]
    args += [x2d, w_out, w_up, w_down, ln]
    in_specs += [row(d), _resident(w_out.shape), _resident(w_up.shape), _resident(w_down.shape),
                 _resident(ln.shape)]
    return pl.pallas_call(
        functools.partial(_tail_kernel, merge=merge, alpha=alpha),
        grid=(m // tm,),
        in_specs=in_specs,
        out_specs=row(d),
        out_shape=jax.ShapeDtypeStruct((m, d), F32),
        scratch_shapes=scratch,
        compiler_params=_cparams(("parallel",)),
        name="tail_attn" if merge else "tail_ret",
    )(*args)


def _rope_tables(seq):
    d = RET_QK_DIM
    inv = 1.0 / (ROPE_BASE ** jnp.linspace(0.0, 1.0, d // 2, dtype=jnp.float32))
    ang = jnp.arange(seq, dtype=jnp.float32)[:, None] * inv[None, :]
    return jnp.cos(ang), jnp.sin(ang)


def _ret_w_in_layout(w_in):
    perm = np.arange(w_in.shape[1])
    half = RET_QK_DIM // 2
    for h in range(2 * RET_HEADS):
        base = h * RET_QK_DIM
        perm[base:base + half] = base + 2 * np.arange(half)
        perm[base + half:base + RET_QK_DIM] = base + 2 * np.arange(half) + 1
    return w_in[:, perm].astype(BF16)


def kernel(x, ret_w_in, ret_w_out, attn_w_in, attn_w_out, rel_bias, ffn_w_up, ffn_w_down, ln_gain, ln_bias):
    batch, seq, d = x.shape
    depth = ffn_w_up.shape[0]
    alpha = (2 * depth) ** 0.25
    cos, sin = _rope_tables(seq)
    bias = _band_bias(rel_bias)
    h = x.reshape(batch * seq, d)
    for i in range(depth):
        j = i // 2
        ln = jnp.stack([ln_gain[i, 0], ln_bias[i, 0], ln_gain[i, 1], ln_bias[i, 1]])
        if i % 2 == 0:
            p = _proj_rope(h, _ret_w_in_layout(ret_w_in[j]), cos, sin, seq)
            mix_in = _retention_core(p, batch, seq)
            w_out = ret_w_out[j]
        else:
            w_in = attn_w_in[j].astype(BF16).reshape(d, 3, ATT_GROUPS, ATT_GROUP_W)
            outs = []
            for g, (_, dil) in enumerate(DILATED_PAIRS):
                p = _proj_attn(h, w_in[:, :, g].reshape(d, 3 * ATT_GROUP_W), dil, batch, seq)
                outs.append(_attn_group(p, bias, g))
            mix_in = tuple(o for o, _ in outs) + tuple(l for _, l in outs)
            w_out = attn_w_out[j]
        h = _tail(mix_in, h, w_out.astype(BF16), ffn_w_up[i].astype(BF16),
                  ffn_w_down[i].astype(BF16), ln, alpha)
    return h.reshape(batch, seq, d)
```

```python
import functools
import math

import numpy as np
import jax
import jax.numpy as jnp
from jax import lax
from jax.experimental import pallas as pl
from jax.experimental.pallas import tpu as pltpu

F32 = jnp.float32
BF16 = jnp.bfloat16

D_MODEL = 1024
RET_QK_DIM = 256
RET_HEADS = D_MODEL // RET_QK_DIM
RET_V_DIM = 2 * RET_QK_DIM
RET_QK = RET_HEADS * RET_QK_DIM
RET_V = RET_HEADS * RET_V_DIM
ROPE_BASE = 10000.0
DILATED_PAIRS = ((128, 1), (512, 4), (2048, 16))
ATT_GROUPS = 3
ATT_HEAD_DIM = 64
ATT_HEADS = D_MODEL // ATT_HEAD_DIM
ATT_GROUP_W = ATT_HEADS * ATT_HEAD_DIM
BAND_BLOCK = 128
NUM_BUCKETS = 32
MAX_DISTANCE = 2048
D_FF = ((8 * D_MODEL + 3 * 256 - 1) // (3 * 256)) * 256
LN_EPS = 1e-5
GN_EPS = 1e-5

LANES = 128
VMEM_LIMIT_BYTES = 56 * 1024 * 1024
RET_CHUNK = 256
PROJ_TM = 1024
PROJ_TN = 1024
ATT_ROWS_MAX = 1024
TAIL_TM = 512
FF_CHUNK = 256
MASK_VALUE = -1e30
LOG2E = math.log2(math.e)
LN2 = math.log(2.0)
QUERY_SCALE = ATT_HEAD_DIM ** -0.5 * LOG2E


def _cparams(sem):
    return pltpu.CompilerParams(dimension_semantics=sem, vmem_limit_bytes=VMEM_LIMIT_BYTES)


def _resident(shape):
    nd = len(shape)
    return pl.BlockSpec(shape, lambda *_: (0,) * nd, pipeline_mode=pl.Buffered(1))


def _proj_rope_kernel(x_ref, w_ref, cos_ref, sin_ref, o_ref, *, tn):
    j = pl.program_id(1)
    acc = jnp.dot(x_ref[...].astype(BF16), w_ref[...], preferred_element_type=F32)

    @pl.when(j < 2)
    def _():
        scale = jnp.where(j == 1, RET_QK_DIM ** -0.5, 1.0).astype(F32)
        c = cos_ref[...] * scale
        s = sin_ref[...] * scale
        half = RET_QK_DIM // 2
        for h in range(tn // RET_QK_DIM):
            lo = h * RET_QK_DIM
            a = acc[:, lo:lo + half]
            b = acc[:, lo + half:lo + 2 * half]
            o_ref[:, lo:lo + half] = (a * c - b * s).astype(BF16)
            o_ref[:, lo + half:lo + 2 * half] = (a * s + b * c).astype(BF16)

    @pl.when(j >= 2)
    def _():
        o_ref[...] = acc.astype(BF16)


def _proj_rope(x2d, w, cos, sin, seq):
    m, k = x2d.shape
    n = w.shape[1]
    tm, tn = PROJ_TM, PROJ_TN
    assert m % tm == 0 and n % tn == 0 and tn == RET_QK and seq % tm == 0
    nseq = seq // tm
    table = pl.BlockSpec((tm, RET_QK_DIM // 2), lambda i, j: (i % nseq, 0))
    return pl.pallas_call(
        functools.partial(_proj_rope_kernel, tn=tn),
        grid=(m // tm, n // tn),
        in_specs=[pl.BlockSpec((tm, k), lambda i, j: (i, 0)),
                  pl.BlockSpec((k, tn), lambda i, j: (0, j)), table, table],
        out_specs=pl.BlockSpec((tm, tn), lambda i, j: (i, j)),
        out_shape=jax.ShapeDtypeStruct((m, n), BF16),
        compiler_params=_cparams(("parallel", "arbitrary")),
        name="proj_rope",
    )(x2d, w, cos, sin)


def _proj_attn_kernel(x_ref, w_ref, wkt_ref, qv_ref, kt_ref, slab_ref, xb_ref, *, dil):
    j = pl.program_id(1)
    tm, k = x_ref.shape
    rows = tm // dil

    @pl.when(j == 0)
    def _():
        if dil == 1:
            xb_ref[...] = x_ref[...].astype(BF16)
        else:
            for s in range(k // LANES):
                slab_ref[...] = x_ref[:, s * LANES:(s + 1) * LANES]
                for r in range(dil):
                    xb_ref[r * rows:(r + 1) * rows, s * LANES:(s + 1) * LANES] = (
                        slab_ref[pl.ds(r, rows, stride=dil), :].astype(BF16))

    @pl.when(j != 1)
    def _():
        acc = jnp.dot(xb_ref[...], w_ref[...], preferred_element_type=F32)
        scale = jnp.where(j == 0, QUERY_SCALE, 1.0).astype(F32)
        res = (acc * scale).astype(BF16)
        for r in range(dil):
            qv_ref[r] = res[r * rows:(r + 1) * rows, :]

    @pl.when(j == 1)
    def _():
        acc_t = lax.dot_general(wkt_ref[...], xb_ref[...], (((1,), (1,)), ((), ())),
                                preferred_element_type=F32).astype(BF16)
        for r in range(dil):
            for c in range(rows // BAND_BLOCK):
                lo = r * rows + c * BAND_BLOCK
                kt_ref[r, c] = acc_t[:, lo:lo + BAND_BLOCK]


def _proj_attn(x2d, w_qv, w_kt, dil, batch, seq):
    m, k = x2d.shape
    w = ATT_GROUP_W
    tm = max(PROJ_TM, dil * BAND_BLOCK)
    assert seq % tm == 0 and tm % (dil * BAND_BLOCK) == 0
    nseq, rows = seq // tm, tm // dil
    return pl.pallas_call(
        functools.partial(_proj_attn_kernel, dil=dil),
        grid=(m // tm, 3),
        in_specs=[pl.BlockSpec((tm, k), lambda i, j: (i, 0)),
                  pl.BlockSpec((k, w), lambda i, j: (0, j // 2)),
                  _resident(w_kt.shape)],
        out_specs=[pl.BlockSpec((None, dil, rows, w), lambda i, j: (i // nseq, 0, i % nseq, j // 2)),
                   pl.BlockSpec((None, dil, rows // BAND_BLOCK, w, BAND_BLOCK),
                                lambda i, j: (i // nseq, 0, i % nseq, 0, 0))],
        out_shape=[jax.ShapeDtypeStruct((batch, dil, seq // dil, 2 * w), BF16),
                   jax.ShapeDtypeStruct((batch, dil, seq // dil // BAND_BLOCK, w, BAND_BLOCK), BF16)],
        scratch_shapes=[pltpu.VMEM((tm, LANES), F32), pltpu.VMEM((tm, k), BF16)],
        compiler_params=_cparams(("parallel", "arbitrary")),
        name=f"proj_attn_d{dil}",
    )(x2d, w_qv, w_kt)


def _ret_kernel(q_ref, k_ref, v_ref, g_ref, intra_ref, xi_ref, zeta_ref, o_ref, state_ref, *, gamma_c):
    @pl.when(pl.program_id(1) == 0)
    def _():
        state_ref[...] = jnp.zeros_like(state_ref)

    dk, dv = RET_QK_DIM, RET_V_DIM
    for h in range(RET_HEADS):
        q = q_ref[:, h * dk:(h + 1) * dk]
        k = k_ref[:, h * dk:(h + 1) * dk]
        v = v_ref[:, h * dv:(h + 1) * dv]
        s = lax.dot_general(q, k, (((1,), (1,)), ((), ())), preferred_element_type=F32)
        s = s * intra_ref[h]
        inner = jnp.dot(s.astype(BF16), v, preferred_element_type=F32)
        st = state_ref[h]
        cross = jnp.dot(q, st.astype(BF16), preferred_element_type=F32) * xi_ref[h]
        kz = (k.astype(F32) * zeta_ref[h]).astype(BF16)
        upd = lax.dot_general(kz, v, (((0,), (0,)), ((), ())), preferred_element_type=F32)
        state_ref[h] = gamma_c[h] * st + upd
        y = inner + cross
        mu = jnp.mean(y, axis=-1, keepdims=True)
        yc = y - mu
        var = jnp.mean(yc * yc, axis=-1, keepdims=True)
        yn = yc * lax.rsqrt(var + GN_EPS)
        g = g_ref[:, h * dv:(h + 1) * dv].astype(F32)
        silu = g * (1.0 / (1.0 + jnp.exp(-g)))
        o_ref[:, h * dv:(h + 1) * dv] = (silu * yn).astype(BF16)


def _ret_tables(c):
    h = np.arange(RET_HEADS, dtype=np.float64)
    log_gamma = np.log(1.0 - 2.0 ** (-5.0 - h))
    idx = np.arange(c, dtype=np.float64)
    diff = idx[:, None] - idx[None, :]
    intra = np.where(diff[None] >= 0, np.exp(np.maximum(diff, 0.0)[None] * log_gamma[:, None, None]), 0.0)
    xi = np.exp((idx[None, :] + 1.0) * log_gamma[:, None])[..., None]
    zeta = np.exp((c - 1.0 - idx[None, :]) * log_gamma[:, None])[..., None]
    gamma_c = tuple(float(np.float32(v)) for v in np.exp(c * log_gamma))
    return (jnp.asarray(intra, F32), jnp.asarray(xi, F32), jnp.asarray(zeta, F32), gamma_c)


def _retention_core(p, batch, seq):
    c = RET_CHUNK
    assert seq % c == 0
    nc = seq // c
    intra, xi, zeta, gamma_c = _ret_tables(c)
    row = lambda b, i: b * nc + i
    return pl.pallas_call(
        functools.partial(_ret_kernel, gamma_c=gamma_c),
        grid=(batch, nc),
        in_specs=[pl.BlockSpec((c, RET_QK), lambda b, i: (row(b, i), 0)),
                  pl.BlockSpec((c, RET_QK), lambda b, i: (row(b, i), 1)),
                  pl.BlockSpec((c, RET_V), lambda b, i: (row(b, i), 1)),
                  pl.BlockSpec((c, RET_V), lambda b, i: (row(b, i), 2)),
                  _resident(intra.shape), _resident(xi.shape), _resident(zeta.shape)],
        out_specs=pl.BlockSpec((c, RET_V), lambda b, i: (row(b, i), 0)),
        out_shape=jax.ShapeDtypeStruct((batch * seq, RET_V), BF16),
        scratch_shapes=[pltpu.VMEM((RET_HEADS, RET_QK_DIM, RET_V_DIM), F32)],
        compiler_params=_cparams(("parallel", "arbitrary")),
        name="retention_core",
    )(p, p, p, p, intra, xi, zeta)


def _t5_bucket(dist):
    max_exact = NUM_BUCKETS // 2
    d_f = jnp.maximum(dist, 1).astype(jnp.float32)
    large = max_exact + (jnp.log(d_f / max_exact) / math.log(MAX_DISTANCE / max_exact)
                         * (NUM_BUCKETS - max_exact)).astype(jnp.int32)
    large = jnp.minimum(large, NUM_BUCKETS - 1)
    return jnp.where(dist < max_exact, dist, large)


def _band_buckets():
    blk = BAND_BLOCK
    qi = jnp.arange(blk)[:, None]
    kj = jnp.arange(2 * blk)[None, :]
    delta = qi + blk - kj
    out = []
    for window, dil in DILATED_PAIRS:
        steps = window // dil
        ok = (delta >= 0) & (delta <= steps)
        out.append(jnp.where(ok, _t5_bucket(jnp.maximum(delta, 0) * dil), -1))
    return jnp.stack(out).astype(jnp.int32)


def _bias_kernel(tab_ref, bkt_ref, o_ref):
    gh = pl.program_id(0)
    bkt = bkt_ref[0]
    acc = jnp.full(bkt.shape, MASK_VALUE, F32)
    for b in range(NUM_BUCKETS):
        acc = jnp.where(bkt == b, tab_ref[gh, b] * LOG2E, acc)
    o_ref[0] = acc


def _band_bias(rel_bias):
    gh = rel_bias.shape[0]
    blk = BAND_BLOCK
    return pl.pallas_call(
        _bias_kernel,
        grid=(gh,),
        in_specs=[pl.BlockSpec(memory_space=pltpu.SMEM),
                  pl.BlockSpec((1, blk, 2 * blk), lambda i: (i // ATT_HEADS, 0, 0))],
        out_specs=pl.BlockSpec((1, blk, 2 * blk), lambda i: (i, 0, 0)),
        out_shape=jax.ShapeDtypeStruct((gh, blk, 2 * blk), F32),
        compiler_params=_cparams(("parallel",)),
        name="band_bias",
    )(rel_bias, _band_buckets())


def _attn_kernel(q_ref, v_ref, vh_ref, kt_ref, kth_ref, bias_ref, o_ref, lse_ref, *, nblk):
    blk, hd = BAND_BLOCK, ATT_HEAD_DIM
    first_prog = pl.program_id(2) == 0
    lane = lax.broadcasted_iota(jnp.int32, (blk, LANES), 1)
    col = lax.broadcasted_iota(jnp.int32, (blk, 2 * blk), 1)
    vlane = lax.broadcasted_iota(jnp.int32, (2 * blk, LANES), 1)
    zeros_k = jnp.zeros((hd, 2 * blk), BF16)

    def block(n, carry):
        row = pl.multiple_of(n * blk, blk)
        pn = jnp.maximum(n - 1, 0)
        prow = pl.multiple_of(pn * blk, blk)
        first_blk = n == 0
        prev_dead = jnp.logical_and(jnp.logical_and(first_blk, first_prog), col < blk)
        lse_tile = jnp.zeros((blk, LANES), F32)
        for hp in range(ATT_HEADS // 2):
            pair = slice(hp * LANES, (hp + 1) * LANES)
            q = q_ref[0, pl.ds(row, blk), pair]
            vp = jnp.where(first_blk, vh_ref[0, :, pair], v_ref[0, pl.ds(prow, blk), pair])
            vb = jnp.concatenate([vp, v_ref[0, pl.ds(row, blk), pair]], axis=0)
            o = None
            for hh in range(2):
                h = 2 * hp + hh
                hrows = slice(h * hd, (h + 1) * hd)
                kp = jnp.where(first_blk, kth_ref[0, 0, hrows, :], kt_ref[0, pn, hrows, :])
                kbt = jnp.concatenate([kp, kt_ref[0, n, hrows, :]], axis=1)
                rhs = jnp.concatenate([kbt, zeros_k] if hh == 0 else [zeros_k, kbt], axis=0)
                s = jnp.dot(q, rhs, preferred_element_type=F32) + bias_ref[h]
                s = jnp.where(prev_dead, MASK_VALUE, s)
                m = jnp.max(s, axis=-1, keepdims=True)
                e = jnp.exp2(s - m)
                den = jnp.sum(e, axis=-1, keepdims=True)
                vsel = jnp.where((vlane < hd) if hh == 0 else (vlane >= hd), vb, jnp.zeros_like(vb))
                part = jnp.dot(e.astype(BF16), vsel, preferred_element_type=F32) * (1.0 / den)
                o = part if o is None else o + part
                lse_tile = jnp.where(lane == h, (m + jnp.log2(den)) * LN2, lse_tile)
            o_ref[0, pl.ds(row, blk), pair] = o.astype(BF16)
        lse_ref[0, pl.ds(row, blk), :] = lse_tile
        return carry

    lax.fori_loop(0, nblk, block, 0)


def _attn_group(qv, kt, bias, g):
    window, dil = DILATED_PAIRS[g]
    assert window // dil == BAND_BLOCK
    blk, w = BAND_BLOCK, ATT_GROUP_W
    batch, _, length, _ = qv.shape
    rows = min(length, ATT_ROWS_MAX)
    assert length % rows == 0 and rows % blk == 0
    nprog, nblk = length // rows, rows // blk
    main = lambda part: pl.BlockSpec((None, 1, rows, w), lambda b, r, s: (b, r, s, part))
    prev_blk = lambda s: jnp.maximum(s * nblk - 1, 0)
    return pl.pallas_call(
        functools.partial(_attn_kernel, nblk=nblk),
        grid=(batch, dil, nprog),
        in_specs=[main(0), main(1),
                  pl.BlockSpec((None, 1, blk, w), lambda b, r, s: (b, r, prev_blk(s), 1)),
                  pl.BlockSpec((None, 1, nblk, w, blk), lambda b, r, s: (b, r, s, 0, 0)),
                  pl.BlockSpec((None, 1, 1, w, blk), lambda b, r, s: (b, r, prev_blk(s), 0, 0)),
                  pl.BlockSpec((ATT_HEADS, blk, 2 * blk), lambda b, r, s: (g, 0, 0))],
        out_specs=[pl.BlockSpec((None, 1, rows, w), lambda b, r, s: (b, r, s, 0)),
                   pl.BlockSpec((None, 1, rows, LANES), lambda b, r, s: (b, r, s, 0))],
        out_shape=[jax.ShapeDtypeStruct((batch, dil, length, w), BF16),
                   jax.ShapeDtypeStruct((batch, dil, length, LANES), F32)],
        compiler_params=_cparams(("parallel", "parallel", "arbitrary")),
        name=f"attn_g{g}",
    )(qv, qv, qv, kt, kt, bias)


def _layer_norm(v, gain, bias):
    mu = jnp.mean(v, axis=-1, keepdims=True)
    vc = v - mu
    var = jnp.mean(vc * vc, axis=-1, keepdims=True)
    return vc * lax.rsqrt(var + LN_EPS) * gain + bias


def _token_order(src_ref, dst_ref):
    dil, rows, width = src_ref.shape
    for r in range(dil):
        for s in range(width // LANES):
            dst_ref[s, pl.ds(r, rows, stride=dil), :] = (
                src_ref[r, :, s * LANES:(s + 1) * LANES].astype(F32))


def _tail_kernel(*refs, merge, alpha):
    if merge:
        (o0_ref, o1_ref, o2_ref, l0_ref, l1_ref, l2_ref, ex_ref, x_ref, wo_ref, wu_ref, wd_ref,
         ln_ref, out_ref, *scratch) = refs
        o_refs, l_refs = (o0_ref, o1_ref, o2_ref), (l0_ref, l1_ref, l2_ref)
        outs, lses, k = [], [], 0
        for o_ref, l_ref in zip(o_refs, l_refs):
            if o_ref.shape[0] == 1:
                outs.append(o_ref[0].astype(F32))
                lses.append(l_ref[0])
            else:
                so_ref, sl_ref = scratch[k], scratch[k + 1]
                k += 2
                _token_order(o_ref, so_ref)
                _token_order(l_ref, sl_ref)
                outs.append(jnp.concatenate([so_ref[s] for s in range(so_ref.shape[0])], axis=1))
                lses.append(sl_ref[0])
        mx = jnp.maximum(jnp.maximum(lses[0], lses[1]), lses[2])
        es = [jnp.exp(l - mx) for l in lses]
        inv = 1.0 / (es[0] + es[1] + es[2])
        z = None
        for e, o in zip(es, outs):
            wgt = (e * inv).astype(BF16)
            term = jnp.dot(wgt, ex_ref[...], preferred_element_type=F32) * o
            z = term if z is None else z + term
        z = z.astype(BF16)
    else:
        z_ref, x_ref, wo_ref, wu_ref, wd_ref, ln_ref, out_ref = refs
        z = z_ref[...]
    mix = jnp.dot(z, wo_ref[...], preferred_element_type=F32)
    x1 = _layer_norm(alpha * x_ref[...] + mix, ln_ref[0:1, :], ln_ref[1:2, :])
    x1b = x1.astype(BF16)
    acc = jnp.zeros(x1.shape, F32)
    for c0 in range(0, D_FF, FF_CHUNK):
        gate = jnp.dot(x1b, wu_ref[:, c0:c0 + FF_CHUNK], preferred_element_type=F32)
        up = jnp.dot(x1b, wu_ref[:, D_FF + c0:D_FF + c0 + FF_CHUNK], preferred_element_type=F32)
        hid = gate * (1.0 / (1.0 + jnp.exp(-gate))) * up
        acc = acc + jnp.dot(hid.astype(BF16), wd_ref[c0:c0 + FF_CHUNK, :], preferred_element_type=F32)
    out_ref[...] = _layer_norm(alpha * x1 + acc, ln_ref[2:3, :], ln_ref[3:4, :])


def _tail(mix_in, x2d, w_out, w_up, w_down, ln, alpha):
    m, d = x2d.shape
    tm = TAIL_TM
    assert m % tm == 0 and D_FF % FF_CHUNK == 0
    merge = isinstance(mix_in, tuple)
    row = lambda width: pl.BlockSpec((tm, width), lambda i: (i, 0))
    scratch = []
    if merge:
        expand = np.zeros((LANES, ATT_GROUP_W), np.float32)
        for h in range(ATT_HEADS):
            expand[h, h * ATT_HEAD_DIM:(h + 1) * ATT_HEAD_DIM] = 1.0
        expand = jnp.asarray(expand, BF16)
        in_specs = []
        for a in mix_in:
            batch, dil, length, width = a.shape
            nseq = dil * length // tm
            assert tm % (dil * 16) == 0
            in_specs.append(pl.BlockSpec((None, dil, tm // dil, width),
                                         lambda i, nseq=nseq: (i // nseq, 0, i % nseq, 0)))
        for a in mix_in[:ATT_GROUPS]:
            if a.shape[1] > 1:
                scratch += [pltpu.VMEM((ATT_GROUP_W // LANES, tm, LANES), F32),
                            pltpu.VMEM((1, tm, LANES), F32)]
        args = list(mix_in) + [expand]
        in_specs.append(_resident(expand.shape))
    else:
        args = [mix_in]
        in_specs = [row(mix_in.shape[1])]
    args += [x2d, w_out, w_up, w_down, ln]
    in_specs += [row(d), _resident(w_out.shape), _resident(w_up.shape), _resident(w_down.shape),
                 _resident(ln.shape)]
    return pl.pallas_call(
        functools.partial(_tail_kernel, merge=merge, alpha=alpha),
        grid=(m // tm,),
        in_specs=in_specs,
        out_specs=row(d),
        out_shape=jax.ShapeDtypeStruct((m, d), F32),
        scratch_shapes=scratch,
        compiler_params=_cparams(("parallel",)),
        name="tail_attn" if merge else "tail_ret",
    )(*args)


def _rope_tables(seq):
    d = RET_QK_DIM
    inv = 1.0 / (ROPE_BASE ** jnp.linspace(0.0, 1.0, d // 2, dtype=jnp.float32))
    ang = jnp.arange(seq, dtype=jnp.float32)[:, None] * inv[None, :]
    return jnp.cos(ang), jnp.sin(ang)


def _ret_w_in_layout(w_in):
    perm = np.arange(w_in.shape[1])
    half = RET_QK_DIM // 2
    for h in range(2 * RET_HEADS):
        base = h * RET_QK_DIM
        perm[base:base + half] = base + 2 * np.arange(half)
        perm[base + half:base + RET_QK_DIM] = base + 2 * np.arange(half) + 1
    return w_in[:, perm].astype(BF16)


def kernel(x, ret_w_in, ret_w_out, attn_w_in, attn_w_out, rel_bias, ffn_w_up, ffn_w_down, ln_gain, ln_bias):
    batch, seq, d = x.shape
    depth = ffn_w_up.shape[0]
    alpha = (2 * depth) ** 0.25
    cos, sin = _rope_tables(seq)
    bias = _band_bias(rel_bias)
    h = x.reshape(batch * seq, d)
    for i in range(depth):
        j = i // 2
        ln = jnp.stack([ln_gain[i, 0], ln_bias[i, 0], ln_gain[i, 1], ln_bias[i, 1]])
        if i % 2 == 0:
            p = _proj_rope(h, _ret_w_in_layout(ret_w_in[j]), cos, sin, seq)
            mix_in = _retention_core(p, batch, seq)
            w_out = ret_w_out[j]
        else:
            w_in = attn_w_in[j].astype(BF16).reshape(d, 3, ATT_GROUPS, ATT_GROUP_W)
            outs = []
            for g, (_, dil) in enumerate(DILATED_PAIRS):
                w_qv = jnp.concatenate([w_in[:, 0, g], w_in[:, 2, g]], axis=1)
                qv, kt = _proj_attn(h, w_qv, w_in[:, 1, g].T, dil, batch, seq)
                outs.append(_attn_group(qv, kt, bias, g))
            mix_in = tuple(o for o, _ in outs) + tuple(l for _, l in outs)
            w_out = attn_w_out[j]
        h = _tail(mix_in, h, w_out.astype(BF16), ffn_w_up[i].astype(BF16),
                  ffn_w_down[i].astype(BF16), ln, alpha)
    return h.reshape(batch, seq, d)
```

```python
import functools
import math

import numpy as np
import jax
import jax.numpy as jnp
from jax import lax
from jax.experimental import pallas as pl
from jax.experimental.pallas import tpu as pltpu

F32 = jnp.float32
BF16 = jnp.bfloat16

D_MODEL = 1024
RET_QK_DIM = 256
RET_HEADS = D_MODEL // RET_QK_DIM
RET_V_DIM = 2 * RET_QK_DIM
RET_QK = RET_HEADS * RET_QK_DIM
RET_V = RET_HEADS * RET_V_DIM
ROPE_BASE = 10000.0
DILATED_PAIRS = ((128, 1), (512, 4), (2048, 16))
ATT_GROUPS = 3
ATT_HEAD_DIM = 64
ATT_HEADS = D_MODEL // ATT_HEAD_DIM
ATT_GROUP_W = ATT_HEADS * ATT_HEAD_DIM
BAND_BLOCK = 128
NUM_BUCKETS = 32
MAX_DISTANCE = 2048
D_FF = ((8 * D_MODEL + 3 * 256 - 1) // (3 * 256)) * 256
LN_EPS = 1e-5
GN_EPS = 1e-5

LANES = 128
VMEM_LIMIT_BYTES = 56 * 1024 * 1024
RET_CHUNK = 256
PROJ_TM = 1024
PROJ_ROPE_TM = 512
PROJ_CHUNK = 512
PROJ_KT_TOKENS = 256
ATT_ROWS_MAX = 1024
TAIL_TM = 512
FF_CHUNK = 256
MASK_VALUE = -1e30
LOG2E = math.log2(math.e)
LN2 = math.log(2.0)
QUERY_SCALE = ATT_HEAD_DIM ** -0.5 * LOG2E


def _cparams(sem):
    return pltpu.CompilerParams(dimension_semantics=sem, vmem_limit_bytes=VMEM_LIMIT_BYTES)


def _resident(shape):
    nd = len(shape)
    return pl.BlockSpec(shape, lambda *_: (0,) * nd, pipeline_mode=pl.Buffered(1))


def _proj_rope_kernel(x_ref, w_ref, cos_ref, sin_ref, o_ref):
    xb = x_ref[...].astype(BF16)
    n = w_ref.shape[1]
    half = RET_QK_DIM // 2
    for part in range(2):
        scale = 1.0 if part == 0 else RET_QK_DIM ** -0.5
        c = cos_ref[...] * scale
        s = sin_ref[...] * scale
        for h in range(RET_HEADS):
            lo = part * RET_QK + h * RET_QK_DIM
            acc = jnp.dot(xb, w_ref[:, lo:lo + RET_QK_DIM], preferred_element_type=F32)
            a, b = acc[:, :half], acc[:, half:]
            o_ref[:, lo:lo + half] = (a * c - b * s).astype(BF16)
            o_ref[:, lo + half:lo + 2 * half] = (a * s + b * c).astype(BF16)
    for lo in range(2 * RET_QK, n, PROJ_CHUNK):
        o_ref[:, lo:lo + PROJ_CHUNK] = jnp.dot(
            xb, w_ref[:, lo:lo + PROJ_CHUNK], preferred_element_type=F32).astype(BF16)


def _proj_rope(x2d, w, cos, sin, seq):
    m, k = x2d.shape
    n = w.shape[1]
    tm = PROJ_ROPE_TM
    assert m % tm == 0 and seq % tm == 0 and (n - 2 * RET_QK) % PROJ_CHUNK == 0
    nseq = seq // tm
    table = pl.BlockSpec((tm, RET_QK_DIM // 2), lambda i: (i % nseq, 0))
    return pl.pallas_call(
        _proj_rope_kernel,
        grid=(m // tm,),
        in_specs=[pl.BlockSpec((tm, k), lambda i: (i, 0)), _resident(w.shape), table, table],
        out_specs=pl.BlockSpec((tm, n), lambda i: (i, 0)),
        out_shape=jax.ShapeDtypeStruct((m, n), BF16),
        compiler_params=_cparams(("parallel",)),
        name="proj_rope",
    )(x2d, w, cos, sin)


def _proj_attn_kernel(x_ref, w_ref, wkt_ref, qv_ref, kt_ref, slab_ref, xb_ref, *, dil):
    tm, k = x_ref.shape
    rows = tm // dil
    w = ATT_GROUP_W
    if dil == 1:
        xb_ref[...] = x_ref[...].astype(BF16)
    else:
        for s in range(k // LANES):
            slab_ref[...] = x_ref[:, s * LANES:(s + 1) * LANES]
            for r in range(dil):
                xb_ref[r * rows:(r + 1) * rows, s * LANES:(s + 1) * LANES] = (
                    slab_ref[pl.ds(r, rows, stride=dil), :].astype(BF16))

    for lo in range(0, 2 * w, PROJ_CHUNK):
        acc = jnp.dot(xb_ref[...], w_ref[:, lo:lo + PROJ_CHUNK], preferred_element_type=F32)
        if lo < w:
            acc = acc * QUERY_SCALE
        res = acc.astype(BF16)
        for r in range(dil):
            qv_ref[r, :, lo:lo + PROJ_CHUNK] = res[r * rows:(r + 1) * rows, :]

    piece = min(rows, BAND_BLOCK)
    fill = pl.program_id(0) % (BAND_BLOCK // piece)
    for t0 in range(0, tm, PROJ_KT_TOKENS):
        acc_t = lax.dot_general(wkt_ref[...], xb_ref[t0:t0 + PROJ_KT_TOKENS, :],
                                (((1,), (1,)), ((), ())), preferred_element_type=F32).astype(BF16)
        for lo in range(0, PROJ_KT_TOKENS, piece):
            r, c = (t0 + lo) // rows, ((t0 + lo) % rows) // BAND_BLOCK
            if piece == BAND_BLOCK:
                kt_ref[r, c] = acc_t[:, lo:lo + piece]
            else:
                for f in range(BAND_BLOCK // piece):
                    @pl.when(fill == f)
                    def _(r=r, lo=lo, f=f):
                        kt_ref[r, 0, :, f * piece:(f + 1) * piece] = acc_t[:, lo:lo + piece]


def _proj_attn(x2d, w_qv, w_kt, dil, batch, seq):
    m, k = x2d.shape
    w = ATT_GROUP_W
    tm = PROJ_TM
    nseq, rows = seq // tm, tm // dil
    piece = min(rows, BAND_BLOCK)
    assert seq % tm == 0 and tm % dil == 0 and rows % piece == 0 and BAND_BLOCK % piece == 0
    assert PROJ_KT_TOKENS % piece == 0 and tm % PROJ_KT_TOKENS == 0 and (2 * w) % PROJ_CHUNK == 0
    tiles_per_blk = BAND_BLOCK // piece
    assert nseq % tiles_per_blk == 0
    kt_blocks = max(rows // BAND_BLOCK, 1)
    return pl.pallas_call(
        functools.partial(_proj_attn_kernel, dil=dil),
        grid=(m // tm,),
        in_specs=[pl.BlockSpec((tm, k), lambda i: (i, 0)),
                  _resident(w_qv.shape), _resident(w_kt.shape)],
        out_specs=[pl.BlockSpec((None, dil, rows, 2 * w), lambda i: (i // nseq, 0, i % nseq, 0)),
                   pl.BlockSpec((None, dil, kt_blocks, w, BAND_BLOCK),
                                lambda i: (i // nseq, 0, (i % nseq) // tiles_per_blk, 0, 0))],
        out_shape=[jax.ShapeDtypeStruct((batch, dil, seq // dil, 2 * w), BF16),
                   jax.ShapeDtypeStruct((batch, dil, seq // dil // BAND_BLOCK, w, BAND_BLOCK), BF16)],
        scratch_shapes=[pltpu.VMEM((tm, LANES), F32), pltpu.VMEM((tm, k), BF16)],
        compiler_params=_cparams(("arbitrary",)),
        name=f"proj_attn_d{dil}",
    )(x2d, w_qv, w_kt)


def _ret_kernel(q_ref, k_ref, v_ref, g_ref, intra_ref, xi_ref, zeta_ref, o_ref, state_ref, *, gamma_c):
    @pl.when(pl.program_id(1) == 0)
    def _():
        state_ref[...] = jnp.zeros_like(state_ref)

    dk, dv = RET_QK_DIM, RET_V_DIM
    for h in range(RET_HEADS):
        q = q_ref[:, h * dk:(h + 1) * dk]
        k = k_ref[:, h * dk:(h + 1) * dk]
        v = v_ref[:, h * dv:(h + 1) * dv]
        s = lax.dot_general(q, k, (((1,), (1,)), ((), ())), preferred_element_type=F32)
        s = s * intra_ref[h]
        inner = jnp.dot(s.astype(BF16), v, preferred_element_type=F32)
        st = state_ref[h]
        cross = jnp.dot(q, st.astype(BF16), preferred_element_type=F32) * xi_ref[h]
        kz = (k.astype(F32) * zeta_ref[h]).astype(BF16)
        upd = lax.dot_general(kz, v, (((0,), (0,)), ((), ())), preferred_element_type=F32)
        state_ref[h] = gamma_c[h] * st + upd
        y = inner + cross
        mu = jnp.mean(y, axis=-1, keepdims=True)
        yc = y - mu
        var = jnp.mean(yc * yc, axis=-1, keepdims=True)
        yn = yc * lax.rsqrt(var + GN_EPS)
        g = g_ref[:, h * dv:(h + 1) * dv].astype(F32)
        silu = g * (1.0 / (1.0 + jnp.exp(-g)))
        o_ref[:, h * dv:(h + 1) * dv] = (silu * yn).astype(BF16)


def _ret_tables(c):
    h = np.arange(RET_HEADS, dtype=np.float64)
    log_gamma = np.log(1.0 - 2.0 ** (-5.0 - h))
    idx = np.arange(c, dtype=np.float64)
    diff = idx[:, None] - idx[None, :]
    intra = np.where(diff[None] >= 0, np.exp(np.maximum(diff, 0.0)[None] * log_gamma[:, None, None]), 0.0)
    xi = np.exp((idx[None, :] + 1.0) * log_gamma[:, None])[..., None]
    zeta = np.exp((c - 1.0 - idx[None, :]) * log_gamma[:, None])[..., None]
    gamma_c = tuple(float(np.float32(v)) for v in np.exp(c * log_gamma))
    return (jnp.asarray(intra, F32), jnp.asarray(xi, F32), jnp.asarray(zeta, F32), gamma_c)


def _retention_core(p, batch, seq):
    c = RET_CHUNK
    assert seq % c == 0
    nc = seq // c
    intra, xi, zeta, gamma_c = _ret_tables(c)
    row = lambda b, i: b * nc + i
    return pl.pallas_call(
        functools.partial(_ret_kernel, gamma_c=gamma_c),
        grid=(batch, nc),
        in_specs=[pl.BlockSpec((c, RET_QK), lambda b, i: (row(b, i), 0)),
                  pl.BlockSpec((c, RET_QK), lambda b, i: (row(b, i), 1)),
                  pl.BlockSpec((c, RET_V), lambda b, i: (row(b, i), 1)),
                  pl.BlockSpec((c, RET_V), lambda b, i: (row(b, i), 2)),
                  _resident(intra.shape), _resident(xi.shape), _resident(zeta.shape)],
        out_specs=pl.BlockSpec((c, RET_V), lambda b, i: (row(b, i), 0)),
        out_shape=jax.ShapeDtypeStruct((batch * seq, RET_V), BF16),
        scratch_shapes=[pltpu.VMEM((RET_HEADS, RET_QK_DIM, RET_V_DIM), F32)],
        compiler_params=_cparams(("parallel", "arbitrary")),
        name="retention_core",
    )(p, p, p, p, intra, xi, zeta)


def _t5_bucket(dist):
    max_exact = NUM_BUCKETS // 2
    d_f = jnp.maximum(dist, 1).astype(jnp.float32)
    large = max_exact + (jnp.log(d_f / max_exact) / math.log(MAX_DISTANCE / max_exact)
                         * (NUM_BUCKETS - max_exact)).astype(jnp.int32)
    large = jnp.minimum(large, NUM_BUCKETS - 1)
    return jnp.where(dist < max_exact, dist, large)


def _band_buckets():
    blk = BAND_BLOCK
    qi = jnp.arange(blk)[:, None]
    kj = jnp.arange(2 * blk)[None, :]
    delta = qi + blk - kj
    out = []
    for window, dil in DILATED_PAIRS:
        steps = window // dil
        ok = (delta >= 0) & (delta <= steps)
        out.append(jnp.where(ok, _t5_bucket(jnp.maximum(delta, 0) * dil), -1))
    return jnp.stack(out).astype(jnp.int32)


def _bias_kernel(tab_ref, bkt_ref, o_ref):
    gh = pl.program_id(0)
    bkt = bkt_ref[0]
    acc = jnp.full(bkt.shape, MASK_VALUE, F32)
    for b in range(NUM_BUCKETS):
        acc = jnp.where(bkt == b, tab_ref[gh, b] * LOG2E, acc)
    o_ref[0] = acc


def _band_bias(rel_bias):
    gh = rel_bias.shape[0]
    blk = BAND_BLOCK
    return pl.pallas_call(
        _bias_kernel,
        grid=(gh,),
        in_specs=[pl.BlockSpec(memory_space=pltpu.SMEM),
                  pl.BlockSpec((1, blk, 2 * blk), lambda i: (i // ATT_HEADS, 0, 0))],
        out_specs=pl.BlockSpec((1, blk, 2 * blk), lambda i: (i, 0, 0)),
        out_shape=jax.ShapeDtypeStruct((gh, blk, 2 * blk), F32),
        compiler_params=_cparams(("parallel",)),
        name="band_bias",
    )(rel_bias, _band_buckets())


def _attn_kernel(q_ref, v_ref, vh_ref, kt_ref, kth_ref, bias_ref, o_ref, lse_ref, *, nblk):
    blk, hd = BAND_BLOCK, ATT_HEAD_DIM
    first_prog = pl.program_id(2) == 0
    lane = lax.broadcasted_iota(jnp.int32, (blk, LANES), 1)
    col = lax.broadcasted_iota(jnp.int32, (blk, 2 * blk), 1)
    vlane = lax.broadcasted_iota(jnp.int32, (2 * blk, LANES), 1)
    zeros_k = jnp.zeros((hd, 2 * blk), BF16)

    def block(n, carry):
        row = pl.multiple_of(n * blk, blk)
        pn = jnp.maximum(n - 1, 0)
        prow = pl.multiple_of(pn * blk, blk)
        first_blk = n == 0
        prev_dead = jnp.logical_and(jnp.logical_and(first_blk, first_prog), col < blk)
        lse_tile = jnp.zeros((blk, LANES), F32)
        for hp in range(ATT_HEADS // 2):
            pair = slice(hp * LANES, (hp + 1) * LANES)
            q = q_ref[0, pl.ds(row, blk), pair]
            vp = jnp.where(first_blk, vh_ref[0, :, pair], v_ref[0, pl.ds(prow, blk), pair])
            vb = jnp.concatenate([vp, v_ref[0, pl.ds(row, blk), pair]], axis=0)
            o = None
            for hh in range(2):
                h = 2 * hp + hh
                hrows = slice(h * hd, (h + 1) * hd)
                kp = jnp.where(first_blk, kth_ref[0, 0, hrows, :], kt_ref[0, pn, hrows, :])
                kbt = jnp.concatenate([kp, kt_ref[0, n, hrows, :]], axis=1)
                rhs = jnp.concatenate([kbt, zeros_k] if hh == 0 else [zeros_k, kbt], axis=0)
                s = jnp.dot(q, rhs, preferred_element_type=F32) + bias_ref[h]
                s = jnp.where(prev_dead, MASK_VALUE, s)
                m = jnp.max(s, axis=-1, keepdims=True)
                e = jnp.exp2(s - m)
                den = jnp.sum(e, axis=-1, keepdims=True)
                vsel = jnp.where((vlane < hd) if hh == 0 else (vlane >= hd), vb, jnp.zeros_like(vb))
                part = jnp.dot(e.astype(BF16), vsel, preferred_element_type=F32) * (1.0 / den)
                o = part if o is None else o + part
                lse_tile = jnp.where(lane == h, (m + jnp.log2(den)) * LN2, lse_tile)
            o_ref[0, pl.ds(row, blk), pair] = o.astype(BF16)
        lse_ref[0, pl.ds(row, blk), :] = lse_tile
        return carry

    lax.fori_loop(0, nblk, block, 0)


def _attn_group(qv, kt, bias, g):
    window, dil = DILATED_PAIRS[g]
    assert window // dil == BAND_BLOCK
    blk, w = BAND_BLOCK, ATT_GROUP_W
    batch, _, length, _ = qv.shape
    rows = min(length, ATT_ROWS_MAX)
    assert length % rows == 0 and rows % blk == 0
    nprog, nblk = length // rows, rows // blk
    main = lambda part: pl.BlockSpec((None, 1, rows, w), lambda b, r, s: (b, r, s, part))
    prev_blk = lambda s: jnp.maximum(s * nblk - 1, 0)
    return pl.pallas_call(
        functools.partial(_attn_kernel, nblk=nblk),
        grid=(batch, dil, nprog),
        in_specs=[main(0), main(1),
                  pl.BlockSpec((None, 1, blk, w), lambda b, r, s: (b, r, prev_blk(s), 1)),
                  pl.BlockSpec((None, 1, nblk, w, blk), lambda b, r, s: (b, r, s, 0, 0)),
                  pl.BlockSpec((None, 1, 1, w, blk), lambda b, r, s: (b, r, prev_blk(s), 0, 0)),
                  pl.BlockSpec((ATT_HEADS, blk, 2 * blk), lambda b, r, s: (g, 0, 0))],
        out_specs=[pl.BlockSpec((None, 1, rows, w), lambda b, r, s: (b, r, s, 0)),
                   pl.BlockSpec((None, 1, rows, LANES), lambda b, r, s: (b, r, s, 0))],
        out_shape=[jax.ShapeDtypeStruct((batch, dil, length, w), BF16),
                   jax.ShapeDtypeStruct((batch, dil, length, LANES), F32)],
        compiler_params=_cparams(("parallel", "parallel", "arbitrary")),
        name=f"attn_g{g}",
    )(qv, qv, qv, kt, kt, bias)


def _layer_norm(v, gain, bias):
    mu = jnp.mean(v, axis=-1, keepdims=True)
    vc = v - mu
    var = jnp.mean(vc * vc, axis=-1, keepdims=True)
    return vc * lax.rsqrt(var + LN_EPS) * gain + bias


def _token_order(src_ref, dst_ref):
    dil, rows, width = src_ref.shape
    for r in range(dil):
        for s in range(width // LANES):
            dst_ref[s, pl.ds(r, rows, stride=dil), :] = (
                src_ref[r, :, s * LANES:(s + 1) * LANES].astype(F32))


def _tail_kernel(*refs, merge, alpha):
    if merge:
        (o0_ref, o1_ref, o2_ref, l0_ref, l1_ref, l2_ref, ex_ref, x_ref, wo_ref, wu_ref, wd_ref,
         ln_ref, out_ref, *scratch) = refs
        o_refs, l_refs = (o0_ref, o1_ref, o2_ref), (l0_ref, l1_ref, l2_ref)
        outs, lses, k = [], [], 0
        for o_ref, l_ref in zip(o_refs, l_refs):
            if o_ref.shape[0] == 1:
                outs.append(o_ref[0].astype(F32))
                lses.append(l_ref[0])
            else:
                so_ref, sl_ref = scratch[k], scratch[k + 1]
                k += 2
                _token_order(o_ref, so_ref)
                _token_order(l_ref, sl_ref)
                outs.append(jnp.concatenate([so_ref[s] for s in range(so_ref.shape[0])], axis=1))
                lses.append(sl_ref[0])
        mx = jnp.maximum(jnp.maximum(lses[0], lses[1]), lses[2])
        es = [jnp.exp(l - mx) for l in lses]
        inv = 1.0 / (es[0] + es[1] + es[2])
        z = None
        for e, o in zip(es, outs):
            wgt = (e * inv).astype(BF16)
            term = jnp.dot(wgt, ex_ref[...], preferred_element_type=F32) * o
            z = term if z is None else z + term
        z = z.astype(BF16)
    else:
        z_ref, x_ref, wo_ref, wu_ref, wd_ref, ln_ref, out_ref = refs
        z = z_ref[...]
    mix = jnp.dot(z, wo_ref[...], preferred_element_type=F32)
    x1 = _layer_norm(alpha * x_ref[...] + mix, ln_ref[0:1, :], ln_ref[1:2, :])
    x1b = x1.astype(BF16)
    acc = jnp.zeros(x1.shape, F32)
    for c0 in range(0, D_FF, FF_CHUNK):
        gate = jnp.dot(x1b, wu_ref[:, c0:c0 + FF_CHUNK], preferred_element_type=F32)
        up = jnp.dot(x1b, wu_ref[:, D_FF + c0:D_FF + c0 + FF_CHUNK], preferred_element_type=F32)
        hid = gate * (1.0 / (1.0 + jnp.exp(-gate))) * up
        acc = acc + jnp.dot(hid.astype(BF16), wd_ref[c0:c0 + FF_CHUNK, :], preferred_element_type=F32)
    out_ref[...] = _layer_norm(alpha * x1 + acc, ln_ref[2:3, :], ln_ref[3:4, :])


def _tail(mix_in, x2d, w_out, w_up, w_down, ln, alpha):
    m, d = x2d.shape
    tm = TAIL_TM
    assert m % tm == 0 and D_FF % FF_CHUNK == 0
    merge = isinstance(mix_in, tuple)
    row = lambda width: pl.BlockSpec((tm, width), lambda i: (i, 0))
    scratch = []
    if merge:
        expand = np.zeros((LANES, ATT_GROUP_W), np.float32)
        for h in range(ATT_HEADS):
            expand[h, h * ATT_HEAD_DIM:(h + 1) * ATT_HEAD_DIM] = 1.0
        expand = jnp.asarray(expand, BF16)
        in_specs = []
        for a in mix_in:
            batch, dil, length, width = a.shape
            nseq = dil * length // tm
            assert tm % (dil * 16) == 0
            in_specs.append(pl.BlockSpec((None, dil, tm // dil, width),
                                         lambda i, nseq=nseq: (i // nseq, 0, i % nseq, 0)))
        for a in mix_in[:ATT_GROUPS]:
            if a.shape[1] > 1:
                scratch += [pltpu.VMEM((ATT_GROUP_W // LANES, tm, LANES), F32),
                            pltpu.VMEM((1, tm, LANES), F32)]
        args = list(mix_in) + [expand]
        in_specs.append(_resident(expand.shape))
    else:
        args = [mix_in]
        in_specs = [row(mix_in.shape[1])]
    args += [x2d, w_out, w_up, w_down, ln]
    in_specs += [row(d), _resident(w_out.shape), _resident(w_up.shape), _resident(w_down.shape),
                 _resident(ln.shape)]
    return pl.pallas_call(
        functools.partial(_tail_kernel, merge=merge, alpha=alpha),
        grid=(m // tm,),
        in_specs=in_specs,
        out_specs=row(d),
        out_shape=jax.ShapeDtypeStruct((m, d), F32),
        scratch_shapes=scratch,
        compiler_params=_cparams(("parallel",)),
        name="tail_attn" if merge else "tail_ret",
    )(*args)


def _rope_tables(seq):
    d = RET_QK_DIM
    inv = 1.0 / (ROPE_BASE ** jnp.linspace(0.0, 1.0, d // 2, dtype=jnp.float32))
    ang = jnp.arange(seq, dtype=jnp.float32)[:, None] * inv[None, :]
    return jnp.cos(ang), jnp.sin(ang)


def _ret_w_in_layout(w_in):
    perm = np.arange(w_in.shape[1])
    half = RET_QK_DIM // 2
    for h in range(2 * RET_HEADS):
        base = h * RET_QK_DIM
        perm[base:base + half] = base + 2 * np.arange(half)
        perm[base + half:base + RET_QK_DIM] = base + 2 * np.arange(half) + 1
    return w_in[:, perm].astype(BF16)


def kernel(x, ret_w_in, ret_w_out, attn_w_in, attn_w_out, rel_bias, ffn_w_up, ffn_w_down, ln_gain, ln_bias):
    batch, seq, d = x.shape
    depth = ffn_w_up.shape[0]
    alpha = (2 * depth) ** 0.25
    cos, sin = _rope_tables(seq)
    bias = _band_bias(rel_bias)
    h = x.reshape(batch * seq, d)
    for i in range(depth):
        j = i // 2
        ln = jnp.stack([ln_gain[i, 0], ln_bias[i, 0], ln_gain[i, 1], ln_bias[i, 1]])
        if i % 2 == 0:
            p = _proj_rope(h, _ret_w_in_layout(ret_w_in[j]), cos, sin, seq)
            mix_in = _retention_core(p, batch, seq)
            w_out = ret_w_out[j]
        else:
            w_in = attn_w_in[j].astype(BF16).reshape(d, 3, ATT_GROUPS, ATT_GROUP_W)
            outs = []
            for g, (_, dil) in enumerate(DILATED_PAIRS):
                w_qv = jnp.concatenate([w_in[:, 0, g], w_in[:, 2, g]], axis=1)
                qv, kt = _proj_attn(h, w_qv, w_in[:, 1, g].T, dil, batch, seq)
                outs.append(_attn_group(qv, kt, bias, g))
            mix_in = tuple(o for o, _ in outs) + tuple(l for _, l in outs)
            w_out = attn_w_out[j]
        h = _tail(mix_in, h, w_out.astype(BF16), ffn_w_up[i].astype(BF16),
                  ffn_w_down[i].astype(BF16), ln, alpha)
    return h.reshape(batch, seq, d)
```

```python
import functools
import math

import numpy as np
import jax
import jax.numpy as jnp
from jax import lax
from jax.experimental import pallas as pl
from jax.experimental.pallas import tpu as pltpu

F32 = jnp.float32
BF16 = jnp.bfloat16

D_MODEL = 1024
RET_QK_DIM = 256
RET_HEADS = D_MODEL // RET_QK_DIM
RET_V_DIM = 2 * RET_QK_DIM
RET_QK = RET_HEADS * RET_QK_DIM
RET_V = RET_HEADS * RET_V_DIM
ROPE_BASE = 10000.0
DILATED_PAIRS = ((128, 1), (512, 4), (2048, 16))
ATT_GROUPS = 3
ATT_HEAD_DIM = 64
ATT_HEADS = D_MODEL // ATT_HEAD_DIM
ATT_GROUP_W = ATT_HEADS * ATT_HEAD_DIM
BAND_BLOCK = 128
NUM_BUCKETS = 32
MAX_DISTANCE = 2048
D_FF = ((8 * D_MODEL + 3 * 256 - 1) // (3 * 256)) * 256
LN_EPS = 1e-5
GN_EPS = 1e-5

LANES = 128
VMEM_LIMIT_BYTES = 56 * 1024 * 1024
RET_CHUNK = 256
PROJ_TM = 1024
PROJ_ROPE_TM = 512
PROJ_CHUNK = 512
PROJ_KT_TOKENS = 256
ATT_ROWS_MAX = 1024
ATT_UNROLL = 4
TAIL_TM = 512
FF_CHUNK = 256
MASK_VALUE = -1e30
LOG2E = math.log2(math.e)
QUERY_SCALE = ATT_HEAD_DIM ** -0.5 * LOG2E


def _cparams(sem):
    return pltpu.CompilerParams(dimension_semantics=sem, vmem_limit_bytes=VMEM_LIMIT_BYTES)


def _resident(shape):
    nd = len(shape)
    return pl.BlockSpec(shape, lambda *_: (0,) * nd, pipeline_mode=pl.Buffered(1))


def _proj_rope_kernel(x_ref, w_ref, cos_ref, sin_ref, o_ref):
    xb = x_ref[...].astype(BF16)
    n = w_ref.shape[1]
    half = RET_QK_DIM // 2
    for part in range(2):
        scale = 1.0 if part == 0 else RET_QK_DIM ** -0.5
        c = cos_ref[...] * scale
        s = sin_ref[...] * scale
        for h in range(RET_HEADS):
            lo = part * RET_QK + h * RET_QK_DIM
            acc = jnp.dot(xb, w_ref[:, lo:lo + RET_QK_DIM], preferred_element_type=F32)
            a, b = acc[:, :half], acc[:, half:]
            o_ref[:, lo:lo + half] = (a * c - b * s).astype(BF16)
            o_ref[:, lo + half:lo + 2 * half] = (a * s + b * c).astype(BF16)
    for lo in range(2 * RET_QK, n, PROJ_CHUNK):
        o_ref[:, lo:lo + PROJ_CHUNK] = jnp.dot(
            xb, w_ref[:, lo:lo + PROJ_CHUNK], preferred_element_type=F32).astype(BF16)


def _proj_rope(x2d, w, cos, sin, seq):
    m, k = x2d.shape
    n = w.shape[1]
    tm = PROJ_ROPE_TM
    assert m % tm == 0 and seq % tm == 0 and (n - 2 * RET_QK) % PROJ_CHUNK == 0
    nseq = seq // tm
    table = pl.BlockSpec((tm, RET_QK_DIM // 2), lambda i: (i % nseq, 0))
    return pl.pallas_call(
        _proj_rope_kernel,
        grid=(m // tm,),
        in_specs=[pl.BlockSpec((tm, k), lambda i: (i, 0)), _resident(w.shape), table, table],
        out_specs=pl.BlockSpec((tm, n), lambda i: (i, 0)),
        out_shape=jax.ShapeDtypeStruct((m, n), BF16),
        compiler_params=_cparams(("parallel",)),
        name="proj_rope",
    )(x2d, w, cos, sin)


def _proj_attn_kernel(x_ref, w_ref, wkt_ref, qv_ref, kt_ref, slab_ref, xb_ref):
    nres, rows = qv_ref.shape[0], qv_ref.shape[1]
    tm, k = xb_ref.shape
    w = ATT_GROUP_W
    if nres == 1:
        xb_ref[...] = x_ref[...].astype(BF16)
    else:
        for s in range(k // LANES):
            slab_ref[...] = x_ref[..., s * LANES:(s + 1) * LANES].reshape(tm, LANES)
            for r in range(nres):
                xb_ref[r * rows:(r + 1) * rows, s * LANES:(s + 1) * LANES] = (
                    slab_ref[pl.ds(r, rows, stride=nres), :].astype(BF16))

    for lo in range(0, 2 * w, PROJ_CHUNK):
        acc = jnp.dot(xb_ref[...], w_ref[:, lo:lo + PROJ_CHUNK], preferred_element_type=F32)
        if lo < w:
            acc = acc * QUERY_SCALE
        res = acc.astype(BF16)
        for r in range(nres):
            qv_ref[r, :, lo:lo + PROJ_CHUNK] = res[r * rows:(r + 1) * rows, :]

    for t0 in range(0, tm, PROJ_KT_TOKENS):
        acc_t = lax.dot_general(wkt_ref[...], xb_ref[t0:t0 + PROJ_KT_TOKENS, :],
                                (((1,), (1,)), ((), ())), preferred_element_type=F32).astype(BF16)
        for lo in range(0, PROJ_KT_TOKENS, BAND_BLOCK):
            r, c = (t0 + lo) // rows, ((t0 + lo) % rows) // BAND_BLOCK
            kt_ref[r, c] = acc_t[:, lo:lo + BAND_BLOCK]


def _proj_attn(x2d, w_qv, w_kt, dil, batch, seq):
    m, k = x2d.shape
    w, blk, tm = ATT_GROUP_W, BAND_BLOCK, PROJ_TM
    length = seq // dil
    nres = min(dil, tm // blk)
    rows = tm // nres
    ngrp, nwin = dil // nres, length // rows
    assert dil % nres == 0 and length % rows == 0 and rows % blk == 0
    assert tm % PROJ_KT_TOKENS == 0 and PROJ_KT_TOKENS % blk == 0 and (2 * w) % PROJ_CHUNK == 0
    where = lambda i: (i // (nwin * ngrp), (i // ngrp) % nwin, i % ngrp)
    if ngrp == 1:
        xv = x2d
        x_spec = pl.BlockSpec((tm, k), lambda i: (i, 0))
    else:
        assert nres % 8 == 0
        xv = x2d.reshape(batch, nwin, rows, ngrp, nres, k)
        x_spec = pl.BlockSpec((None, None, rows, None, nres, k),
                              lambda i: (where(i)[0], where(i)[1], 0, where(i)[2], 0, 0))
    return pl.pallas_call(
        _proj_attn_kernel,
        grid=(m // tm,),
        in_specs=[x_spec, _resident(w_qv.shape), _resident(w_kt.shape)],
        out_specs=[pl.BlockSpec((None, nres, rows, 2 * w),
                                lambda i: (where(i)[0], where(i)[2], where(i)[1], 0)),
                   pl.BlockSpec((None, nres, rows // blk, w, blk),
                                lambda i: (where(i)[0], where(i)[2], where(i)[1], 0, 0))],
        out_shape=[jax.ShapeDtypeStruct((batch, dil, length, 2 * w), BF16),
                   jax.ShapeDtypeStruct((batch, dil, length // blk, w, blk), BF16)],
        scratch_shapes=[pltpu.VMEM((tm, LANES), F32), pltpu.VMEM((tm, k), BF16)],
        compiler_params=_cparams(("parallel",)),
        name=f"proj_attn_d{dil}",
    )(x2d if ngrp == 1 else xv, w_qv, w_kt)


def _ret_kernel(q_ref, k_ref, v_ref, g_ref, intra_ref, xi_ref, zeta_ref, o_ref, state_ref, *, gamma_c):
    @pl.when(pl.program_id(1) == 0)
    def _():
        state_ref[...] = jnp.zeros_like(state_ref)

    dk, dv = RET_QK_DIM, RET_V_DIM
    for h in range(RET_HEADS):
        q = q_ref[:, h * dk:(h + 1) * dk]
        k = k_ref[:, h * dk:(h + 1) * dk]
        v = v_ref[:, h * dv:(h + 1) * dv]
        s = lax.dot_general(q, k, (((1,), (1,)), ((), ())), preferred_element_type=F32)
        s = s * intra_ref[h]
        inner = jnp.dot(s.astype(BF16), v, preferred_element_type=F32)
        st = state_ref[h]
        cross = jnp.dot(q, st.astype(BF16), preferred_element_type=F32) * xi_ref[h]
        kz = (k.astype(F32) * zeta_ref[h]).astype(BF16)
        upd = lax.dot_general(kz, v, (((0,), (0,)), ((), ())), preferred_element_type=F32)
        state_ref[h] = gamma_c[h] * st + upd
        y = inner + cross
        mu = jnp.mean(y, axis=-1, keepdims=True)
        yc = y - mu
        var = jnp.mean(yc * yc, axis=-1, keepdims=True)
        yn = yc * lax.rsqrt(var + GN_EPS)
        g = g_ref[:, h * dv:(h + 1) * dv].astype(F32)
        silu = g * (1.0 / (1.0 + jnp.exp(-g)))
        o_ref[:, h * dv:(h + 1) * dv] = (silu * yn).astype(BF16)


def _ret_tables(c):
    h = np.arange(RET_HEADS, dtype=np.float64)
    log_gamma = np.log(1.0 - 2.0 ** (-5.0 - h))
    idx = np.arange(c, dtype=np.float64)
    diff = idx[:, None] - idx[None, :]
    intra = np.where(diff[None] >= 0, np.exp(np.maximum(diff, 0.0)[None] * log_gamma[:, None, None]), 0.0)
    xi = np.exp((idx[None, :] + 1.0) * log_gamma[:, None])[..., None]
    zeta = np.exp((c - 1.0 - idx[None, :]) * log_gamma[:, None])[..., None]
    gamma_c = tuple(float(np.float32(v)) for v in np.exp(c * log_gamma))
    return (jnp.asarray(intra, F32), jnp.asarray(xi, F32), jnp.asarray(zeta, F32), gamma_c)


def _retention_core(p, batch, seq):
    c = RET_CHUNK
    assert seq % c == 0
    nc = seq // c
    intra, xi, zeta, gamma_c = _ret_tables(c)
    row = lambda b, i: b * nc + i
    return pl.pallas_call(
        functools.partial(_ret_kernel, gamma_c=gamma_c),
        grid=(batch, nc),
        in_specs=[pl.BlockSpec((c, RET_QK), lambda b, i: (row(b, i), 0)),
                  pl.BlockSpec((c, RET_QK), lambda b, i: (row(b, i), 1)),
                  pl.BlockSpec((c, RET_V), lambda b, i: (row(b, i), 1)),
                  pl.BlockSpec((c, RET_V), lambda b, i: (row(b, i), 2)),
                  _resident(intra.shape), _resident(xi.shape), _resident(zeta.shape)],
        out_specs=pl.BlockSpec((c, RET_V), lambda b, i: (row(b, i), 0)),
        out_shape=jax.ShapeDtypeStruct((batch * seq, RET_V), BF16),
        scratch_shapes=[pltpu.VMEM((RET_HEADS, RET_QK_DIM, RET_V_DIM), F32)],
        compiler_params=_cparams(("parallel", "arbitrary")),
        name="retention_core",
    )(p, p, p, p, intra, xi, zeta)


def _t5_bucket(dist):
    max_exact = NUM_BUCKETS // 2
    d_f = jnp.maximum(dist, 1).astype(jnp.float32)
    large = max_exact + (jnp.log(d_f / max_exact) / math.log(MAX_DISTANCE / max_exact)
                         * (NUM_BUCKETS - max_exact)).astype(jnp.int32)
    large = jnp.minimum(large, NUM_BUCKETS - 1)
    return jnp.where(dist < max_exact, dist, large)


def _band_buckets():
    blk = BAND_BLOCK
    qi = jnp.arange(blk)[:, None]
    kj = jnp.arange(2 * blk)[None, :]
    delta = qi + blk - kj
    out = []
    for window, dil in DILATED_PAIRS:
        steps = window // dil
        ok = (delta >= 0) & (delta <= steps)
        out.append(jnp.where(ok, _t5_bucket(jnp.maximum(delta, 0) * dil), -1))
    return jnp.stack(out).astype(jnp.int32)


def _bias_kernel(tab_ref, bkt_ref, o_ref):
    gh = pl.program_id(0)
    bkt = bkt_ref[0]
    acc = jnp.full(bkt.shape, MASK_VALUE, F32)
    for b in range(NUM_BUCKETS):
        acc = jnp.where(bkt == b, tab_ref[gh, b] * LOG2E, acc)
    o_ref[0, 0] = acc
    col = lax.broadcasted_iota(jnp.int32, bkt.shape, 1)
    o_ref[1, 0] = jnp.where(col < BAND_BLOCK, MASK_VALUE, acc)


def _band_bias(rel_bias):
    gh = rel_bias.shape[0]
    blk = BAND_BLOCK
    return pl.pallas_call(
        _bias_kernel,
        grid=(gh,),
        in_specs=[pl.BlockSpec(memory_space=pltpu.SMEM),
                  pl.BlockSpec((1, blk, 2 * blk), lambda i: (i // ATT_HEADS, 0, 0))],
        out_specs=pl.BlockSpec((2, 1, blk, 2 * blk), lambda i: (0, i, 0, 0)),
        out_shape=jax.ShapeDtypeStruct((2, gh, blk, 2 * blk), F32),
        compiler_params=_cparams(("parallel",)),
        name="band_bias",
    )(rel_bias, _band_buckets())


def _stat_lane(h):
    return h // 2 + (ATT_HEAD_DIM if h % 2 == 0 else 0)


def _attn_kernel(q_ref, v_ref, vh_ref, kt_ref, kth_ref, bias_ref, o_ref, m_ref, d_ref, *, nblk):
    blk, hd = BAND_BLOCK, ATT_HEAD_DIM
    first_prog = pl.program_id(2) == 0
    lane = lax.broadcasted_iota(jnp.int32, (blk, LANES), 1)
    vlane = lax.broadcasted_iota(jnp.int32, (2 * blk, LANES), 1)
    zeros_k = jnp.zeros((hd, 2 * blk), BF16)
    ones_v = jnp.ones((2 * blk, LANES), BF16)

    def block(t, carry):
        r, n = lax.div(t, nblk), lax.rem(t, nblk)
        row = pl.multiple_of(n * blk, blk)
        pn = jnp.maximum(n - 1, 0)
        prow = pl.multiple_of(pn * blk, blk)
        first_blk = n == 0
        table = jnp.logical_and(first_blk, first_prog).astype(jnp.int32)
        m_tile = jnp.zeros((blk, LANES), F32)
        d_tile = jnp.ones((blk, LANES), F32)
        for hp in range(ATT_HEADS // 2):
            pair = slice(hp * LANES, (hp + 1) * LANES)
            q = q_ref[r, pl.ds(row, blk), pair]
            vp = jnp.where(first_blk, vh_ref[r, :, pair], v_ref[r, pl.ds(prow, blk), pair])
            vb = jnp.concatenate([vp, v_ref[r, pl.ds(row, blk), pair]], axis=0)
            parts = []
            for hh in range(2):
                h = 2 * hp + hh
                hrows = slice(h * hd, (h + 1) * hd)
                kp = jnp.where(first_blk, kth_ref[r, 0, hrows, :], kt_ref[r, pn, hrows, :])
                kbt = jnp.concatenate([kp, kt_ref[r, n, hrows, :]], axis=1)
                rhs = jnp.concatenate([kbt, zeros_k] if hh == 0 else [zeros_k, kbt], axis=0)
                s = jnp.dot(q, rhs, preferred_element_type=F32) + bias_ref[table, h]
                m = jnp.max(s, axis=-1, keepdims=True)
                e = jnp.exp2(s - m).astype(BF16)
                own = (vlane < hd) if hh == 0 else (vlane >= hd)
                part = jnp.dot(e, jnp.where(own, vb, ones_v), preferred_element_type=F32)
                stat = lane == _stat_lane(h)
                m_tile = jnp.where(stat, m, m_tile)
                d_tile = jnp.where(stat, part, d_tile)
                parts.append(part)
            o_ref[r, pl.ds(row, blk), pair] = jnp.where(lane < hd, parts[0], parts[1]).astype(BF16)
        m_ref[r, pl.ds(row, blk), :] = m_tile
        d_ref[r, pl.ds(row, blk), :] = d_tile
        return carry

    lax.fori_loop(0, q_ref.shape[0] * nblk, block, 0, unroll=ATT_UNROLL)


def _attn_group(qv, kt, bias, g):
    window, dil = DILATED_PAIRS[g]
    assert window // dil == BAND_BLOCK
    blk, w = BAND_BLOCK, ATT_GROUP_W
    batch, _, length, _ = qv.shape
    rows = min(length, ATT_ROWS_MAX)
    nres = min(dil, ATT_ROWS_MAX // rows)
    assert length % rows == 0 and rows % blk == 0 and dil % nres == 0
    nprog, nblk = length // rows, rows // blk
    main = lambda part: pl.BlockSpec((None, nres, rows, w), lambda b, r, s: (b, r, s, part))
    prev_blk = lambda s: jnp.maximum(s * nblk - 1, 0)
    return pl.pallas_call(
        functools.partial(_attn_kernel, nblk=nblk),
        grid=(batch, dil // nres, nprog),
        in_specs=[main(0), main(1),
                  pl.BlockSpec((None, nres, blk, w), lambda b, r, s: (b, r, prev_blk(s), 1)),
                  pl.BlockSpec((None, nres, nblk, w, blk), lambda b, r, s: (b, r, s, 0, 0)),
                  pl.BlockSpec((None, nres, 1, w, blk), lambda b, r, s: (b, r, prev_blk(s), 0, 0)),
                  pl.BlockSpec((2, ATT_HEADS, blk, 2 * blk), lambda b, r, s: (0, g, 0, 0),
                               pipeline_mode=pl.Buffered(1))],
        out_specs=[pl.BlockSpec((None, nres, rows, w), lambda b, r, s: (b, r, s, 0)),
                   pl.BlockSpec((None, nres, rows, LANES), lambda b, r, s: (b, r, s, 0)),
                   pl.BlockSpec((None, nres, rows, LANES), lambda b, r, s: (b, r, s, 0))],
        out_shape=[jax.ShapeDtypeStruct((batch, dil, length, w), BF16),
                   jax.ShapeDtypeStruct((batch, dil, length, LANES), F32),
                   jax.ShapeDtypeStruct((batch, dil, length, LANES), F32)],
        compiler_params=_cparams(("parallel", "parallel", "arbitrary")),
        name=f"attn_g{g}",
    )(qv, qv, qv, kt, kt, bias)


def _layer_norm(v, gain, bias):
    mu = jnp.mean(v, axis=-1, keepdims=True)
    vc = v - mu
    var = jnp.mean(vc * vc, axis=-1, keepdims=True)
    return vc * lax.rsqrt(var + LN_EPS) * gain + bias


def _token_order(src_ref, dst_ref):
    dil, rows, width = src_ref.shape
    for r in range(dil):
        for s in range(width // LANES):
            dst_ref[s, pl.ds(r, rows, stride=dil), :] = (
                src_ref[r, :, s * LANES:(s + 1) * LANES].astype(F32))


def _tail_kernel(*refs, merge, alpha):
    if merge:
        g = ATT_GROUPS
        o_refs, m_refs, d_refs = refs[:g], refs[g:2 * g], refs[2 * g:3 * g]
        ex_ref, x_ref, wo_ref, wu_ref, wd_ref, ln_ref, out_ref, *scratch = refs[3 * g:]
        nums, maxs, dens, k = [], [], [], 0
        for o_ref, m_ref, d_ref in zip(o_refs, m_refs, d_refs):
            if o_ref.shape[0] == 1:
                nums.append(o_ref[0].astype(F32))
                maxs.append(m_ref[0])
                dens.append(d_ref[0])
            else:
                so_ref, sm_ref, sd_ref = scratch[k:k + 3]
                k += 3
                _token_order(o_ref, so_ref)
                _token_order(m_ref, sm_ref)
                _token_order(d_ref, sd_ref)
                nums.append(jnp.concatenate([so_ref[s] for s in range(so_ref.shape[0])], axis=1))
                maxs.append(sm_ref[0])
                dens.append(sd_ref[0])
        mx = jnp.maximum(jnp.maximum(maxs[0], maxs[1]), maxs[2])
        es = [jnp.exp2(m - mx) for m in maxs]
        inv = 1.0 / (es[0] * dens[0] + es[1] * dens[1] + es[2] * dens[2])
        z = None
        for e, num in zip(es, nums):
            wgt = (e * inv).astype(BF16)
            term = jnp.dot(wgt, ex_ref[...], preferred_element_type=F32) * num
            z = term if z is None else z + term
        z = z.astype(BF16)
    else:
        z_ref, x_ref, wo_ref, wu_ref, wd_ref, ln_ref, out_ref = refs
        z = z_ref[...]
    mix = jnp.dot(z, wo_ref[...], preferred_element_type=F32)
    x1 = _layer_norm(alpha * x_ref[...] + mix, ln_ref[0:1, :], ln_ref[1:2, :])
    x1b = x1.astype(BF16)
    acc = jnp.zeros(x1.shape, F32)
    for c0 in range(0, D_FF, FF_CHUNK):
        gate = jnp.dot(x1b, wu_ref[:, c0:c0 + FF_CHUNK], preferred_element_type=F32)
        up = jnp.dot(x1b, wu_ref[:, D_FF + c0:D_FF + c0 + FF_CHUNK], preferred_element_type=F32)
        hid = gate * (1.0 / (1.0 + jnp.exp(-gate))) * up
        acc = acc + jnp.dot(hid.astype(BF16), wd_ref[c0:c0 + FF_CHUNK, :], preferred_element_type=F32)
    out_ref[...] = _layer_norm(alpha * x1 + acc, ln_ref[2:3, :], ln_ref[3:4, :])


def _tail(mix_in, x2d, w_out, w_up, w_down, ln, alpha):
    m, d = x2d.shape
    tm = TAIL_TM
    assert m % tm == 0 and D_FF % FF_CHUNK == 0
    merge = isinstance(mix_in, tuple)
    row = lambda width: pl.BlockSpec((tm, width), lambda i: (i, 0))
    scratch = []
    if merge:
        expand = np.zeros((LANES, ATT_GROUP_W), np.float32)
        for h in range(ATT_HEADS):
            expand[_stat_lane(h), h * ATT_HEAD_DIM:(h + 1) * ATT_HEAD_DIM] = 1.0
        expand = jnp.asarray(expand, BF16)
        in_specs = []
        for a in mix_in:
            batch, dil, length, width = a.shape
            nseq = dil * length // tm
            assert tm % (dil * 16) == 0
            in_specs.append(pl.BlockSpec((None, dil, tm // dil, width),
                                         lambda i, nseq=nseq: (i // nseq, 0, i % nseq, 0)))
        for a in mix_in[:ATT_GROUPS]:
            if a.shape[1] > 1:
                scratch += [pltpu.VMEM((ATT_GROUP_W // LANES, tm, LANES), F32),
                            pltpu.VMEM((1, tm, LANES), F32), pltpu.VMEM((1, tm, LANES), F32)]
        args = list(mix_in) + [expand]
        in_specs.append(_resident(expand.shape))
    else:
        args = [mix_in]
        in_specs = [row(mix_in.shape[1])]
    args += [x2d, w_out, w_up, w_down, ln]
    in_specs += [row(d), _resident(w_out.shape), _resident(w_up.shape), _resident(w_down.shape),
                 _resident(ln.shape)]
    return pl.pallas_call(
        functools.partial(_tail_kernel, merge=merge, alpha=alpha),
        grid=(m // tm,),
        in_specs=in_specs,
        out_specs=row(d),
        out_shape=jax.ShapeDtypeStruct((m, d), F32),
        scratch_shapes=scratch,
        compiler_params=_cparams(("parallel",)),
        name="tail_attn" if merge else "tail_ret",
    )(*args)


def _rope_tables(seq):
    d = RET_QK_DIM
    inv = 1.0 / (ROPE_BASE ** jnp.linspace(0.0, 1.0, d // 2, dtype=jnp.float32))
    ang = jnp.arange(seq, dtype=jnp.float32)[:, None] * inv[None, :]
    return jnp.cos(ang), jnp.sin(ang)


def _ret_w_in_layout(w_in):
    perm = np.arange(w_in.shape[1])
    half = RET_QK_DIM // 2
    for h in range(2 * RET_HEADS):
        base = h * RET_QK_DIM
        perm[base:base + half] = base + 2 * np.arange(half)
        perm[base + half:base + RET_QK_DIM] = base + 2 * np.arange(half) + 1
    return w_in[:, perm].astype(BF16)


def kernel(x, ret_w_in, ret_w_out, attn_w_in, attn_w_out, rel_bias, ffn_w_up, ffn_w_down, ln_gain, ln_bias):
    batch, seq, d = x.shape
    depth = ffn_w_up.shape[0]
    alpha = (2 * depth) ** 0.25
    cos, sin = _rope_tables(seq)
    bias = _band_bias(rel_bias)
    h = x.reshape(batch * seq, d)
    for i in range(depth):
        j = i // 2
        ln = jnp.stack([ln_gain[i, 0], ln_bias[i, 0], ln_gain[i, 1], ln_bias[i, 1]])
        if i % 2 == 0:
            p = _proj_rope(h, _ret_w_in_layout(ret_w_in[j]), cos, sin, seq)
            mix_in = _retention_core(p, batch, seq)
            w_out = ret_w_out[j]
        else:
            w_in = attn_w_in[j].astype(BF16).reshape(d, 3, ATT_GROUPS, ATT_GROUP_W)
            outs = []
            for g, (_, dil) in enumerate(DILATED_PAIRS):
                w_qv = jnp.concatenate([w_in[:, 0, g], w_in[:, 2, g]], axis=1)
                qv, kt = _proj_attn(h, w_qv, w_in[:, 1, g].T, dil, batch, seq)
                outs.append(_attn_group(qv, kt, bias, g))
            mix_in = tuple(a[0] for a in outs) + tuple(a[1] for a in outs) + tuple(a[2] for a in outs)
            w_out = attn_w_out[j]
        h = _tail(mix_in, h, w_out.astype(BF16), ffn_w_up[i].astype(BF16),
                  ffn_w_down[i].astype(BF16), ln, alpha)
    return h.reshape(batch, seq, d)
```

```python
import functools
import math

import numpy as np
import jax
import jax.numpy as jnp
from jax import lax
from jax.experimental import pallas as pl
from jax.experimental.pallas import tpu as pltpu

F32 = jnp.float32
BF16 = jnp.bfloat16

D_MODEL = 1024
RET_QK_DIM = 256
RET_HEADS = D_MODEL // RET_QK_DIM
RET_V_DIM = 2 * RET_QK_DIM
RET_QK = RET_HEADS * RET_QK_DIM
RET_V = RET_HEADS * RET_V_DIM
ROPE_BASE = 10000.0
DILATED_PAIRS = ((128, 1), (512, 4), (2048, 16))
ATT_GROUPS = 3
ATT_HEAD_DIM = 64
ATT_HEADS = D_MODEL // ATT_HEAD_DIM
ATT_GROUP_W = ATT_HEADS * ATT_HEAD_DIM
BAND_BLOCK = 128
NUM_BUCKETS = 32
MAX_DISTANCE = 2048
D_FF = ((8 * D_MODEL + 3 * 256 - 1) // (3 * 256)) * 256
LN_EPS = 1e-5
GN_EPS = 1e-5

LANES = 128
VMEM_LIMIT_BYTES = 56 * 1024 * 1024
RET_CHUNK = 256
RET_CHUNKS_PER_STEP = 2
PROJ_TM = 1024
PROJ_ROPE_TM = 512
PROJ_CHUNK = 512
PROJ_KT_TOKENS = 256
ATT_ROWS_MAX = 1024
ATT_UNROLL = 4
TAIL_TM = 512
CAST_ROWS = 256
FF_CHUNK = 256
MASK_VALUE = -1e30
LOG2E = math.log2(math.e)
QUERY_SCALE = ATT_HEAD_DIM ** -0.5 * LOG2E


def _cparams(sem):
    return pltpu.CompilerParams(dimension_semantics=sem, vmem_limit_bytes=VMEM_LIMIT_BYTES)


def _resident(shape, lead=()):
    nd = len(shape)
    return pl.BlockSpec((None,) * len(lead) + tuple(shape), lambda *_: tuple(lead) + (0,) * nd,
                        pipeline_mode=pl.Buffered(1))


def _cast_kernel(w_ref, o_ref):
    o_ref[...] = w_ref[...].astype(BF16)


def _cast_bf16(w):
    layers, r, c = w.shape
    tr = CAST_ROWS
    assert r % tr == 0
    spec = pl.BlockSpec((None, tr, c), lambda l, i: (l, i, 0))
    return pl.pallas_call(
        _cast_kernel, grid=(layers, r // tr), in_specs=[spec], out_specs=spec,
        out_shape=jax.ShapeDtypeStruct(w.shape, BF16),
        compiler_params=_cparams(("parallel", "parallel")), name="cast_bf16",
    )(w)


def _ret_w_in_kernel(w_ref, perm_ref, o_ref):
    wb = w_ref[...].astype(BF16)
    for h in range(2 * RET_HEADS):
        cols = slice(h * RET_QK_DIM, (h + 1) * RET_QK_DIM)
        o_ref[:, cols] = jnp.dot(wb[:, cols], perm_ref[...], preferred_element_type=F32).astype(BF16)
    o_ref[:, 2 * RET_QK:] = wb[:, 2 * RET_QK:]


def _ret_w_in_prep(w):
    layers, r, c = w.shape
    tr = CAST_ROWS
    assert r % tr == 0
    d = RET_QK_DIM
    perm = np.zeros((d, d), np.float32)
    perm[2 * np.arange(d // 2), np.arange(d // 2)] = 1.0
    perm[2 * np.arange(d // 2) + 1, d // 2 + np.arange(d // 2)] = 1.0
    spec = pl.BlockSpec((None, tr, c), lambda l, i: (l, i, 0))
    return pl.pallas_call(
        _ret_w_in_kernel, grid=(layers, r // tr),
        in_specs=[spec, pl.BlockSpec((d, d), lambda l, i: (0, 0))], out_specs=spec,
        out_shape=jax.ShapeDtypeStruct(w.shape, BF16),
        compiler_params=_cparams(("parallel", "parallel")), name="ret_w_in_prep",
    )(w, jnp.asarray(perm, BF16))


def _attn_w_in_kernel(q_ref, k_ref, v_ref, qv_ref, kt_ref):
    w = ATT_GROUP_W
    qv_ref[:, :w] = q_ref[...].astype(BF16)
    qv_ref[:, w:] = v_ref[...].astype(BF16)
    kt_ref[...] = k_ref[...].T.astype(BF16)


def _attn_w_in_prep(w):
    layers, d, _ = w.shape
    g, gw = ATT_GROUPS, ATT_GROUP_W
    part = lambda p: pl.BlockSpec((None, d, gw), lambda l, i: (l, 0, p * g + i))
    return pl.pallas_call(
        _attn_w_in_kernel, grid=(layers, g),
        in_specs=[part(0), part(1), part(2)],
        out_specs=[pl.BlockSpec((None, None, d, 2 * gw), lambda l, i: (l, i, 0, 0)),
                   pl.BlockSpec((None, None, gw, d), lambda l, i: (l, i, 0, 0))],
        out_shape=[jax.ShapeDtypeStruct((layers, g, d, 2 * gw), BF16),
                   jax.ShapeDtypeStruct((layers, g, gw, d), BF16)],
        compiler_params=_cparams(("parallel", "parallel")), name="attn_w_in_prep",
    )(w, w, w)


def _proj_rope_kernel(x_ref, w_ref, cos_ref, sin_ref, o_ref):
    xb = x_ref[...].astype(BF16)
    n = w_ref.shape[1]
    half = RET_QK_DIM // 2
    for part in range(2):
        scale = 1.0 if part == 0 else RET_QK_DIM ** -0.5
        c = cos_ref[...] * scale
        s = sin_ref[...] * scale
        for h in range(RET_HEADS):
            lo = part * RET_QK + h * RET_QK_DIM
            acc = jnp.dot(xb, w_ref[:, lo:lo + RET_QK_DIM], preferred_element_type=F32)
            a, b = acc[:, :half], acc[:, half:]
            o_ref[:, lo:lo + half] = (a * c - b * s).astype(BF16)
            o_ref[:, lo + half:lo + 2 * half] = (a * s + b * c).astype(BF16)
    for lo in range(2 * RET_QK, n, PROJ_CHUNK):
        o_ref[:, lo:lo + PROJ_CHUNK] = jnp.dot(
            xb, w_ref[:, lo:lo + PROJ_CHUNK], preferred_element_type=F32).astype(BF16)


def _proj_rope(x2d, w, layer, cos, sin, seq):
    m, k = x2d.shape
    n = w.shape[2]
    tm = PROJ_ROPE_TM
    assert m % tm == 0 and seq % tm == 0 and (n - 2 * RET_QK) % PROJ_CHUNK == 0
    nseq = seq // tm
    table = pl.BlockSpec((tm, RET_QK_DIM // 2), lambda i: (i % nseq, 0))
    return pl.pallas_call(
        _proj_rope_kernel,
        grid=(m // tm,),
        in_specs=[pl.BlockSpec((tm, k), lambda i: (i, 0)), _resident(w.shape[1:], (layer,)),
                  table, table],
        out_specs=pl.BlockSpec((tm, n), lambda i: (i, 0)),
        out_shape=jax.ShapeDtypeStruct((m, n), BF16),
        compiler_params=_cparams(("parallel",)),
        name="proj_rope",
    )(x2d, w, cos, sin)


def _proj_attn_kernel(x_ref, w_ref, wkt_ref, qv_ref, kt_ref, slab_ref, xb_ref):
    nres, rows = qv_ref.shape[0], qv_ref.shape[1]
    tm, k = xb_ref.shape
    w = ATT_GROUP_W
    if nres == 1:
        xb_ref[...] = x_ref[...].astype(BF16)
    else:
        for s in range(k // LANES):
            slab_ref[...] = x_ref[..., s * LANES:(s + 1) * LANES].reshape(tm, LANES)
            for r in range(nres):
                xb_ref[r * rows:(r + 1) * rows, s * LANES:(s + 1) * LANES] = (
                    slab_ref[pl.ds(r, rows, stride=nres), :].astype(BF16))

    for lo in range(0, 2 * w, PROJ_CHUNK):
        acc = jnp.dot(xb_ref[...], w_ref[:, lo:lo + PROJ_CHUNK], preferred_element_type=F32)
        if lo < w:
            acc = acc * QUERY_SCALE
        res = acc.astype(BF16)
        for r in range(nres):
            qv_ref[r, :, lo:lo + PROJ_CHUNK] = res[r * rows:(r + 1) * rows, :]

    for t0 in range(0, tm, PROJ_KT_TOKENS):
        acc_t = lax.dot_general(wkt_ref[...], xb_ref[t0:t0 + PROJ_KT_TOKENS, :],
                                (((1,), (1,)), ((), ())), preferred_element_type=F32).astype(BF16)
        for lo in range(0, PROJ_KT_TOKENS, BAND_BLOCK):
            r, c = (t0 + lo) // rows, ((t0 + lo) % rows) // BAND_BLOCK
            kt_ref[r, c] = acc_t[:, lo:lo + BAND_BLOCK]


def _proj_attn(x2d, w_qv, w_kt, layer, g, dil, batch, seq):
    m, k = x2d.shape
    w, blk, tm = ATT_GROUP_W, BAND_BLOCK, PROJ_TM
    length = seq // dil
    nres = min(dil, tm // blk)
    rows = tm // nres
    ngrp, nwin = dil // nres, length // rows
    assert dil % nres == 0 and length % rows == 0 and rows % blk == 0
    assert tm % PROJ_KT_TOKENS == 0 and PROJ_KT_TOKENS % blk == 0 and (2 * w) % PROJ_CHUNK == 0
    where = lambda i: (i // (nwin * ngrp), (i // ngrp) % nwin, i % ngrp)
    if ngrp == 1:
        xv = x2d
        x_spec = pl.BlockSpec((tm, k), lambda i: (i, 0))
    else:
        assert nres % 8 == 0
        xv = x2d.reshape(batch, nwin, rows, ngrp, nres, k)
        x_spec = pl.BlockSpec((None, None, rows, None, nres, k),
                              lambda i: (where(i)[0], where(i)[1], 0, where(i)[2], 0, 0))
    return pl.pallas_call(
        _proj_attn_kernel,
        grid=(m // tm,),
        in_specs=[x_spec, _resident(w_qv.shape[2:], (layer, g)), _resident(w_kt.shape[2:], (layer, g))],
        out_specs=[pl.BlockSpec((None, nres, rows, 2 * w),
                                lambda i: (where(i)[0], where(i)[2], where(i)[1], 0)),
                   pl.BlockSpec((None, nres, rows // blk, w, blk),
                                lambda i: (where(i)[0], where(i)[2], where(i)[1], 0, 0))],
        out_shape=[jax.ShapeDtypeStruct((batch, dil, length, 2 * w), BF16),
                   jax.ShapeDtypeStruct((batch, dil, length // blk, w, blk), BF16)],
        scratch_shapes=[pltpu.VMEM((tm, LANES), F32), pltpu.VMEM((tm, k), BF16)],
        compiler_params=_cparams(("parallel",)),
        name=f"proj_attn_d{dil}",
    )(x2d if ngrp == 1 else xv, w_qv, w_kt)


def _ret_kernel(q_ref, k_ref, v_ref, g_ref, intra_ref, xi_ref, zeta_ref, o_ref, state_ref, *, gamma_c):
    @pl.when(pl.program_id(1) == 0)
    def _():
        state_ref[...] = jnp.zeros_like(state_ref)

    dk, dv, c = RET_QK_DIM, RET_V_DIM, RET_CHUNK
    for r0 in range(0, q_ref.shape[0], c):
        rows = slice(r0, r0 + c)
        for h in range(RET_HEADS):
            q = q_ref[rows, h * dk:(h + 1) * dk]
            k = k_ref[rows, h * dk:(h + 1) * dk]
            v = v_ref[rows, h * dv:(h + 1) * dv]
            s = lax.dot_general(q, k, (((1,), (1,)), ((), ())), preferred_element_type=F32)
            s = (s * intra_ref[h]).astype(BF16)
            st = state_ref[h]
            qx = (q.astype(F32) * xi_ref[h]).astype(BF16)
            y = jnp.dot(jnp.concatenate([s, qx], axis=1),
                        jnp.concatenate([v, st.astype(BF16)], axis=0), preferred_element_type=F32)
            kz = (k.astype(F32) * zeta_ref[h]).astype(BF16)
            upd = lax.dot_general(kz, v, (((0,), (0,)), ((), ())), preferred_element_type=F32)
            state_ref[h] = gamma_c[h] * st + upd
            mu = jnp.mean(y, axis=-1, keepdims=True)
            yc = y - mu
            var = jnp.mean(yc * yc, axis=-1, keepdims=True)
            yn = yc * lax.rsqrt(var + GN_EPS)
            g = g_ref[rows, h * dv:(h + 1) * dv].astype(F32)
            silu = g * (0.5 * jnp.tanh(0.5 * g) + 0.5)
            o_ref[rows, h * dv:(h + 1) * dv] = (silu * yn).astype(BF16)


def _ret_tables(c):
    h = np.arange(RET_HEADS, dtype=np.float64)
    log_gamma = np.log(1.0 - 2.0 ** (-5.0 - h))
    idx = np.arange(c, dtype=np.float64)
    diff = idx[:, None] - idx[None, :]
    intra = np.where(diff[None] >= 0, np.exp(np.maximum(diff, 0.0)[None] * log_gamma[:, None, None]), 0.0)
    xi = np.exp((idx[None, :] + 1.0) * log_gamma[:, None])[..., None]
    zeta = np.exp((c - 1.0 - idx[None, :]) * log_gamma[:, None])[..., None]
    gamma_c = tuple(float(np.float32(v)) for v in np.exp(c * log_gamma))
    return (jnp.asarray(intra, F32), jnp.asarray(xi, F32), jnp.asarray(zeta, F32), gamma_c)


def _retention_core(p, batch, seq):
    c = RET_CHUNK * RET_CHUNKS_PER_STEP
    assert seq % c == 0
    nc = seq // c
    intra, xi, zeta, gamma_c = _ret_tables(RET_CHUNK)
    row = lambda b, i: b * nc + i
    return pl.pallas_call(
        functools.partial(_ret_kernel, gamma_c=gamma_c),
        grid=(batch, nc),
        in_specs=[pl.BlockSpec((c, RET_QK), lambda b, i: (row(b, i), 0)),
                  pl.BlockSpec((c, RET_QK), lambda b, i: (row(b, i), 1)),
                  pl.BlockSpec((c, RET_V), lambda b, i: (row(b, i), 1)),
                  pl.BlockSpec((c, RET_V), lambda b, i: (row(b, i), 2)),
                  _resident(intra.shape), _resident(xi.shape), _resident(zeta.shape)],
        out_specs=pl.BlockSpec((c, RET_V), lambda b, i: (row(b, i), 0)),
        out_shape=jax.ShapeDtypeStruct((batch * seq, RET_V), BF16),
        scratch_shapes=[pltpu.VMEM((RET_HEADS, RET_QK_DIM, RET_V_DIM), F32)],
        compiler_params=_cparams(("parallel", "arbitrary")),
        name="retention_core",
    )(p, p, p, p, intra, xi, zeta)


def _t5_bucket(dist):
    max_exact = NUM_BUCKETS // 2
    d_f = jnp.maximum(dist, 1).astype(jnp.float32)
    large = max_exact + (jnp.log(d_f / max_exact) / math.log(MAX_DISTANCE / max_exact)
                         * (NUM_BUCKETS - max_exact)).astype(jnp.int32)
    large = jnp.minimum(large, NUM_BUCKETS - 1)
    return jnp.where(dist < max_exact, dist, large)


def _band_buckets():
    blk = BAND_BLOCK
    qi = jnp.arange(blk)[:, None]
    kj = jnp.arange(2 * blk)[None, :]
    delta = qi + blk - kj
    out = []
    for window, dil in DILATED_PAIRS:
        steps = window // dil
        ok = (delta >= 0) & (delta <= steps)
        out.append(jnp.where(ok, _t5_bucket(jnp.maximum(delta, 0) * dil), -1))
    return jnp.stack(out).astype(jnp.int32)


def _bias_kernel(tab_ref, bkt_ref, o_ref):
    gh = pl.program_id(0)
    bkt = bkt_ref[0]
    acc = jnp.full(bkt.shape, MASK_VALUE, F32)
    for b in range(NUM_BUCKETS):
        acc = jnp.where(bkt == b, tab_ref[gh, b] * LOG2E, acc)
    o_ref[0, 0] = acc
    col = lax.broadcasted_iota(jnp.int32, bkt.shape, 1)
    o_ref[1, 0] = jnp.where(col < BAND_BLOCK, MASK_VALUE, acc)


def _band_bias(rel_bias):
    gh = rel_bias.shape[0]
    blk = BAND_BLOCK
    return pl.pallas_call(
        _bias_kernel,
        grid=(gh,),
        in_specs=[pl.BlockSpec(memory_space=pltpu.SMEM),
                  pl.BlockSpec((1, blk, 2 * blk), lambda i: (i // ATT_HEADS, 0, 0))],
        out_specs=pl.BlockSpec((2, 1, blk, 2 * blk), lambda i: (0, i, 0, 0)),
        out_shape=jax.ShapeDtypeStruct((2, gh, blk, 2 * blk), F32),
        compiler_params=_cparams(("parallel",)),
        name="band_bias",
    )(rel_bias, _band_buckets())


def _stat_lane(h):
    return h // 2 + (ATT_HEAD_DIM if h % 2 == 0 else 0)


def _attn_kernel(q_ref, v_ref, vh_ref, kt_ref, kth_ref, bias_ref, o_ref, m_ref, d_ref, *, nblk):
    blk, hd = BAND_BLOCK, ATT_HEAD_DIM
    first_prog = pl.program_id(2) == 0
    lane = lax.broadcasted_iota(jnp.int32, (blk, LANES), 1)
    vlane = lax.broadcasted_iota(jnp.int32, (2 * blk, LANES), 1)
    zeros_k = jnp.zeros((hd, 2 * blk), BF16)
    ones_v = jnp.ones((2 * blk, LANES), BF16)

    def block(t, carry):
        r, n = lax.div(t, nblk), lax.rem(t, nblk)
        row = pl.multiple_of(n * blk, blk)
        pn = jnp.maximum(n - 1, 0)
        prow = pl.multiple_of(pn * blk, blk)
        first_blk = n == 0
        table = jnp.logical_and(first_blk, first_prog).astype(jnp.int32)
        m_tile = jnp.zeros((blk, LANES), F32)
        d_tile = jnp.ones((blk, LANES), F32)
        for hp in range(ATT_HEADS // 2):
            pair = slice(hp * LANES, (hp + 1) * LANES)
            q = q_ref[r, pl.ds(row, blk), pair]
            vp = jnp.where(first_blk, vh_ref[r, :, pair], v_ref[r, pl.ds(prow, blk), pair])
            vb = jnp.concatenate([vp, v_ref[r, pl.ds(row, blk), pair]], axis=0)
            parts = []
            for hh in range(2):
                h = 2 * hp + hh
                hrows = slice(h * hd, (h + 1) * hd)
                kp = jnp.where(first_blk, kth_ref[r, 0, hrows, :], kt_ref[r, pn, hrows, :])
                kbt = jnp.concatenate([kp, kt_ref[r, n, hrows, :]], axis=1)
                rhs = jnp.concatenate([kbt, zeros_k] if hh == 0 else [zeros_k, kbt], axis=0)
                s = jnp.dot(q, rhs, preferred_element_type=F32) + bias_ref[table, h]
                m = jnp.max(s, axis=-1, keepdims=True)
                e = jnp.exp2(s - m).astype(BF16)
                own = (vlane < hd) if hh == 0 else (vlane >= hd)
                part = jnp.dot(e, jnp.where(own, vb, ones_v), preferred_element_type=F32)
                stat = lane == _stat_lane(h)
                m_tile = jnp.where(stat, m, m_tile)
                d_tile = jnp.where(stat, part, d_tile)
                parts.append(part)
            o_ref[r, pl.ds(row, blk), pair] = jnp.where(lane < hd, parts[0], parts[1]).astype(BF16)
        m_ref[r, pl.ds(row, blk), :] = m_tile
        d_ref[r, pl.ds(row, blk), :] = d_tile
        return carry

    lax.fori_loop(0, q_ref.shape[0] * nblk, block, 0, unroll=ATT_UNROLL)


def _attn_group(qv, kt, bias, g):
    window, dil = DILATED_PAIRS[g]
    assert window // dil == BAND_BLOCK
    blk, w = BAND_BLOCK, ATT_GROUP_W
    batch, _, length, _ = qv.shape
    rows = min(length, ATT_ROWS_MAX)
    nres = min(dil, ATT_ROWS_MAX // rows)
    assert length % rows == 0 and rows % blk == 0 and dil % nres == 0
    nprog, nblk = length // rows, rows // blk
    main = lambda part: pl.BlockSpec((None, nres, rows, w), lambda b, r, s: (b, r, s, part))
    prev_blk = lambda s: jnp.maximum(s * nblk - 1, 0)
    return pl.pallas_call(
        functools.partial(_attn_kernel, nblk=nblk),
        grid=(batch, dil // nres, nprog),
        in_specs=[main(0), main(1),
                  pl.BlockSpec((None, nres, blk, w), lambda b, r, s: (b, r, prev_blk(s), 1)),
                  pl.BlockSpec((None, nres, nblk, w, blk), lambda b, r, s: (b, r, s, 0, 0)),
                  pl.BlockSpec((None, nres, 1, w, blk), lambda b, r, s: (b, r, prev_blk(s), 0, 0)),
                  pl.BlockSpec((2, ATT_HEADS, blk, 2 * blk), lambda b, r, s: (0, g, 0, 0),
                               pipeline_mode=pl.Buffered(1))],
        out_specs=[pl.BlockSpec((None, nres, rows, w), lambda b, r, s: (b, r, s, 0)),
                   pl.BlockSpec((None, nres, rows, LANES), lambda b, r, s: (b, r, s, 0)),
                   pl.BlockSpec((None, nres, rows, LANES), lambda b, r, s: (b, r, s, 0))],
        out_shape=[jax.ShapeDtypeStruct((batch, dil, length, w), BF16),
                   jax.ShapeDtypeStruct((batch, dil, length, LANES), F32),
                   jax.ShapeDtypeStruct((batch, dil, length, LANES), F32)],
        compiler_params=_cparams(("parallel", "parallel", "arbitrary")),
        name=f"attn_g{g}",
    )(qv, qv, qv, kt, kt, bias)


def _layer_norm(v, gain, bias):
    mu = jnp.mean(v, axis=-1, keepdims=True)
    vc = v - mu
    var = jnp.mean(vc * vc, axis=-1, keepdims=True)
    return vc * lax.rsqrt(var + LN_EPS) * gain + bias


def _token_order(src_ref, dst_ref):
    dil, rows, width = src_ref.shape
    for r in range(dil):
        for s in range(width // LANES):
            dst_ref[s, pl.ds(r, rows, stride=dil), :] = (
                src_ref[r, :, s * LANES:(s + 1) * LANES].astype(F32))


def _merge_groups(o_refs, m_refs, d_refs, ex_ref, scratch):
    nums, maxs, dens = [], [], []
    for o_ref, m_ref, d_ref in zip(o_refs, m_refs, d_refs):
        if o_ref.shape[0] == 1:
            nums.append(o_ref[0].astype(F32))
            maxs.append(m_ref[0])
            dens.append(d_ref[0])
        else:
            so_ref, sm_ref, sd_ref = (scratch.pop(0) for _ in range(3))
            _token_order(o_ref, so_ref)
            _token_order(m_ref, sm_ref)
            _token_order(d_ref, sd_ref)
            nums.append(jnp.concatenate([so_ref[s] for s in range(so_ref.shape[0])], axis=1))
            maxs.append(sm_ref[0])
            dens.append(sd_ref[0])
    mx = jnp.maximum(jnp.maximum(maxs[0], maxs[1]), maxs[2])
    es = [jnp.exp2(m - mx) for m in maxs]
    inv = 1.0 / (es[0] * dens[0] + es[1] * dens[1] + es[2] * dens[2])
    z = None
    for e, num in zip(es, nums):
        wgt = (e * inv).astype(BF16)
        term = jnp.dot(wgt, ex_ref[...], preferred_element_type=F32) * num
        z = term if z is None else z + term
    return z.astype(BF16)


def _tail_kernel(*refs, merge, alpha):
    if merge:
        g = ATT_GROUPS
        o_refs, m_refs, d_refs = refs[:g], refs[g:2 * g], refs[2 * g:3 * g]
        ex_ref, x_ref, wo_ref, wu_ref, wd_ref, gain_ref, bias_ref, out_ref, *scratch = refs[3 * g:]
    else:
        z_ref, x_ref, wo_ref, wu_ref, wd_ref, gain_ref, bias_ref, out_ref = refs
    z = _merge_groups(o_refs, m_refs, d_refs, ex_ref, scratch) if merge else z_ref[...]
    mix = jnp.dot(z, wo_ref[...], preferred_element_type=F32)
    x1 = _layer_norm(alpha * x_ref[...] + mix, gain_ref[0:1, :], bias_ref[0:1, :])
    x1b = x1.astype(BF16)
    acc = jnp.zeros(x1.shape, F32)
    for c0 in range(0, D_FF, FF_CHUNK):
        gate = jnp.dot(x1b, wu_ref[:, c0:c0 + FF_CHUNK], preferred_element_type=F32)
        up = jnp.dot(x1b, wu_ref[:, D_FF + c0:D_FF + c0 + FF_CHUNK], preferred_element_type=F32)
        hid = gate * (1.0 / (1.0 + jnp.exp(-gate))) * up
        acc = acc + jnp.dot(hid.astype(BF16), wd_ref[c0:c0 + FF_CHUNK, :], preferred_element_type=F32)
    out_ref[...] = _layer_norm(alpha * x1 + acc, gain_ref[1:2, :], bias_ref[1:2, :])


def _tail(mix_in, x2d, w_out, mixer_layer, w_up, w_down, ln_gain, ln_bias, layer, alpha):
    m, d = x2d.shape
    tm = TAIL_TM
    assert m % tm == 0 and D_FF % FF_CHUNK == 0
    merge = isinstance(mix_in, tuple)
    row = lambda width: pl.BlockSpec((tm, width), lambda i: (i, 0))
    scratch = []
    if merge:
        expand = np.zeros((LANES, ATT_GROUP_W), np.float32)
        for h in range(ATT_HEADS):
            expand[_stat_lane(h), h * ATT_HEAD_DIM:(h + 1) * ATT_HEAD_DIM] = 1.0
        expand = jnp.asarray(expand, BF16)
        in_specs = []
        for a in mix_in:
            batch, dil, length, width = a.shape
            nseq = dil * length // tm
            assert tm % (dil * 16) == 0
            in_specs.append(pl.BlockSpec((None, dil, tm // dil, width),
                                         lambda i, nseq=nseq: (i // nseq, 0, i % nseq, 0)))
        for a in mix_in[:ATT_GROUPS]:
            if a.shape[1] > 1:
                scratch += [pltpu.VMEM((ATT_GROUP_W // LANES, tm, LANES), F32),
                            pltpu.VMEM((1, tm, LANES), F32), pltpu.VMEM((1, tm, LANES), F32)]
        args = list(mix_in) + [expand]
        in_specs.append(_resident(expand.shape))
    else:
        args = [mix_in]
        in_specs = [row(mix_in.shape[1])]
    args += [x2d, w_out, w_up, w_down, ln_gain, ln_bias]
    in_specs += [row(d), _resident(w_out.shape[1:], (mixer_layer,)),
                 _resident(w_up.shape[1:], (layer,)), _resident(w_down.shape[1:], (layer,)),
                 _resident(ln_gain.shape[1:], (layer,)), _resident(ln_bias.shape[1:], (layer,))]
    return pl.pallas_call(
        functools.partial(_tail_kernel, merge=merge, alpha=alpha),
        grid=(m // tm,),
        in_specs=in_specs,
        out_specs=row(d),
        out_shape=jax.ShapeDtypeStruct((m, d), F32),
        scratch_shapes=scratch,
        compiler_params=_cparams(("parallel",)),
        name="tail_attn" if merge else "tail_ret",
    )(*args)


def _rope_tables(seq):
    d = RET_QK_DIM
    inv = 1.0 / (ROPE_BASE ** jnp.linspace(0.0, 1.0, d // 2, dtype=jnp.float32))
    ang = jnp.arange(seq, dtype=jnp.float32)[:, None] * inv[None, :]
    return jnp.cos(ang), jnp.sin(ang)


def kernel(x, ret_w_in, ret_w_out, attn_w_in, attn_w_out, rel_bias, ffn_w_up, ffn_w_down, ln_gain, ln_bias):
    batch, seq, d = x.shape
    depth = ffn_w_up.shape[0]
    alpha = (2 * depth) ** 0.25
    cos, sin = _rope_tables(seq)
    bias = _band_bias(rel_bias)
    ret_in, ret_out = _ret_w_in_prep(ret_w_in), _cast_bf16(ret_w_out)
    att_qv, att_kt = _attn_w_in_prep(attn_w_in)
    att_out = _cast_bf16(attn_w_out)
    w_up, w_down = _cast_bf16(ffn_w_up), _cast_bf16(ffn_w_down)
    h = x.reshape(batch * seq, d)
    for i in range(depth):
        j = i // 2
        if i % 2 == 0:
            p = _proj_rope(h, ret_in, j, cos, sin, seq)
            mix_in, w_out = _retention_core(p, batch, seq), ret_out
        else:
            outs = []
            for g, (_, dil) in enumerate(DILATED_PAIRS):
                qv, kt = _proj_attn(h, att_qv, att_kt, j, g, dil, batch, seq)
                outs.append(_attn_group(qv, kt, bias, g))
            mix_in = tuple(a[0] for a in outs) + tuple(a[1] for a in outs) + tuple(a[2] for a in outs)
            w_out = att_out
        h = _tail(mix_in, h, w_out, j, w_up, w_down, ln_gain, ln_bias, i, alpha)
    return h.reshape(batch, seq, d)
```

```python
import functools
import math

import numpy as np
import jax
import jax.numpy as jnp
from jax import lax
from jax.experimental import pallas as pl
from jax.experimental.pallas import tpu as pltpu

F32 = jnp.float32
BF16 = jnp.bfloat16

D_MODEL = 1024
RET_QK_DIM = 256
RET_HEADS = D_MODEL // RET_QK_DIM
RET_V_DIM = 2 * RET_QK_DIM
RET_QK = RET_HEADS * RET_QK_DIM
RET_V = RET_HEADS * RET_V_DIM
ROPE_BASE = 10000.0
DILATED_PAIRS = ((128, 1), (512, 4), (2048, 16))
ATT_GROUPS = 3
ATT_HEAD_DIM = 64
ATT_HEADS = D_MODEL // ATT_HEAD_DIM
ATT_GROUP_W = ATT_HEADS * ATT_HEAD_DIM
BAND_BLOCK = 128
NUM_BUCKETS = 32
MAX_DISTANCE = 2048
D_FF = ((8 * D_MODEL + 3 * 256 - 1) // (3 * 256)) * 256
LN_EPS = 1e-5
GN_EPS = 1e-5

LANES = 128
VMEM_LIMIT_BYTES = 56 * 1024 * 1024
RET_CHUNK = 256
RET_CHUNKS_PER_STEP = 2
PROJ_TM = 1024
PROJ_ROPE_TM = 512
PROJ_CHUNK = 512
PROJ_KT_TOKENS = 256
ATT_ROWS_MAX = 1024
ATT_UNROLL = 8
TAIL_TM = 512
CAST_ROWS = 256
FF_CHUNK = 256
MASK_VALUE = -1e30
LOG2E = math.log2(math.e)
QUERY_SCALE = ATT_HEAD_DIM ** -0.5 * LOG2E


def _cparams(sem):
    return pltpu.CompilerParams(dimension_semantics=sem, vmem_limit_bytes=VMEM_LIMIT_BYTES)


def _resident(shape, lead=()):
    nd = len(shape)
    return pl.BlockSpec((None,) * len(lead) + tuple(shape), lambda *_: tuple(lead) + (0,) * nd,
                        pipeline_mode=pl.Buffered(1))


def _cast_kernel(w_ref, o_ref):
    o_ref[...] = w_ref[...].astype(BF16)


def _cast_bf16(w):
    layers, r, c = w.shape
    tr = CAST_ROWS
    assert r % tr == 0
    spec = pl.BlockSpec((None, tr, c), lambda l, i: (l, i, 0))
    return pl.pallas_call(
        _cast_kernel, grid=(layers, r // tr), in_specs=[spec], out_specs=spec,
        out_shape=jax.ShapeDtypeStruct(w.shape, BF16),
        compiler_params=_cparams(("parallel", "parallel")), name="cast_bf16",
    )(w)


def _ret_w_in_kernel(w_ref, perm_ref, o_ref):
    wb = w_ref[...].astype(BF16)
    for h in range(2 * RET_HEADS):
        cols = slice(h * RET_QK_DIM, (h + 1) * RET_QK_DIM)
        o_ref[:, cols] = jnp.dot(wb[:, cols], perm_ref[...], preferred_element_type=F32).astype(BF16)
    o_ref[:, 2 * RET_QK:] = wb[:, 2 * RET_QK:]


def _ret_w_in_prep(w):
    layers, r, c = w.shape
    tr = CAST_ROWS
    assert r % tr == 0
    d = RET_QK_DIM
    perm = np.zeros((d, d), np.float32)
    perm[2 * np.arange(d // 2), np.arange(d // 2)] = 1.0
    perm[2 * np.arange(d // 2) + 1, d // 2 + np.arange(d // 2)] = 1.0
    spec = pl.BlockSpec((None, tr, c), lambda l, i: (l, i, 0))
    return pl.pallas_call(
        _ret_w_in_kernel, grid=(layers, r // tr),
        in_specs=[spec, pl.BlockSpec((d, d), lambda l, i: (0, 0))], out_specs=spec,
        out_shape=jax.ShapeDtypeStruct(w.shape, BF16),
        compiler_params=_cparams(("parallel", "parallel")), name="ret_w_in_prep",
    )(w, jnp.asarray(perm, BF16))


def _attn_w_in_kernel(q_ref, k_ref, v_ref, qv_ref, kt_ref):
    w = ATT_GROUP_W
    qv_ref[:, :w] = q_ref[...].astype(BF16)
    qv_ref[:, w:] = v_ref[...].astype(BF16)
    kt_ref[...] = k_ref[...].T.astype(BF16)


def _attn_w_in_prep(w):
    layers, d, _ = w.shape
    g, gw = ATT_GROUPS, ATT_GROUP_W
    part = lambda p: pl.BlockSpec((None, d, gw), lambda l, i: (l, 0, p * g + i))
    return pl.pallas_call(
        _attn_w_in_kernel, grid=(layers, g),
        in_specs=[part(0), part(1), part(2)],
        out_specs=[pl.BlockSpec((None, None, d, 2 * gw), lambda l, i: (l, i, 0, 0)),
                   pl.BlockSpec((None, None, gw, d), lambda l, i: (l, i, 0, 0))],
        out_shape=[jax.ShapeDtypeStruct((layers, g, d, 2 * gw), BF16),
                   jax.ShapeDtypeStruct((layers, g, gw, d), BF16)],
        compiler_params=_cparams(("parallel", "parallel")), name="attn_w_in_prep",
    )(w, w, w)


def _proj_rope_kernel(x_ref, w_ref, cos_ref, sin_ref, xi_ref, zeta_ref, o_ref):
    xb = x_ref[...].astype(BF16)
    n = w_ref.shape[1]
    half = RET_QK_DIM // 2
    for lo in range(2 * RET_QK + RET_V, n, PROJ_CHUNK):
        acc = jnp.dot(xb, w_ref[:, lo:lo + PROJ_CHUNK], preferred_element_type=F32)
        acc = acc * (0.5 * jnp.tanh(0.5 * acc) + 0.5)
        o_ref[:, lo + 2 * RET_QK:lo + 2 * RET_QK + PROJ_CHUNK] = acc.astype(BF16)
    for part, decay_ref in ((0, xi_ref), (1, zeta_ref)):
        scale = 1.0 if part == 0 else RET_QK_DIM ** -0.5
        c = cos_ref[...] * scale
        s = sin_ref[...] * scale
        for c0 in range(0, RET_QK, PROJ_CHUNK):
            acc = jnp.dot(xb, w_ref[:, part * RET_QK + c0:part * RET_QK + c0 + PROJ_CHUNK],
                          preferred_element_type=F32)
            for h0 in range(0, PROJ_CHUNK, RET_QK_DIM):
                h = (c0 + h0) // RET_QK_DIM
                a, b = acc[:, h0:h0 + half], acc[:, h0 + half:h0 + 2 * half]
                ra, rb = a * c - b * s, a * s + b * c
                lo = part * RET_QK + c0 + h0
                o_ref[:, lo:lo + half] = ra.astype(BF16)
                o_ref[:, lo + half:lo + 2 * half] = rb.astype(BF16)
                lo += 2 * RET_QK
                o_ref[:, lo:lo + half] = (ra * decay_ref[h]).astype(BF16)
                o_ref[:, lo + half:lo + 2 * half] = (rb * decay_ref[h]).astype(BF16)
    for lo in range(2 * RET_QK, 2 * RET_QK + RET_V, PROJ_CHUNK):
        o_ref[:, lo + 2 * RET_QK:lo + 2 * RET_QK + PROJ_CHUNK] = jnp.dot(
            xb, w_ref[:, lo:lo + PROJ_CHUNK], preferred_element_type=F32).astype(BF16)


def _proj_rope(x2d, w, layer, cos, sin, seq):
    m, k = x2d.shape
    n = w.shape[2]
    tm = PROJ_ROPE_TM
    assert m % tm == 0 and seq % tm == 0 and (n - 2 * RET_QK) % PROJ_CHUNK == 0
    assert tm % RET_CHUNK == 0
    nseq = seq // tm
    half = RET_QK_DIM // 2
    table = pl.BlockSpec((tm, half), lambda i: (i % nseq, 0))
    _, xi, zeta, _ = _ret_tables(RET_CHUNK)
    spread = lambda t: jnp.broadcast_to(jnp.tile(t, (1, tm // RET_CHUNK, 1)), (RET_HEADS, tm, half))
    xi_t, zeta_t = spread(xi), spread(zeta)
    return pl.pallas_call(
        _proj_rope_kernel,
        grid=(m // tm,),
        in_specs=[pl.BlockSpec((tm, k), lambda i: (i, 0)), _resident(w.shape[1:], (layer,)),
                  table, table, _resident(xi_t.shape), _resident(zeta_t.shape)],
        out_specs=pl.BlockSpec((tm, n + 2 * RET_QK), lambda i: (i, 0)),
        out_shape=jax.ShapeDtypeStruct((m, n + 2 * RET_QK), BF16),
        compiler_params=_cparams(("parallel",)),
        name="proj_rope",
    )(x2d, w, cos, sin, xi_t, zeta_t)


def _proj_attn_kernel(x_ref, w_ref, wkt_ref, qv_ref, kt_ref, slab_ref, xb_ref):
    nres, rows = qv_ref.shape[0], qv_ref.shape[1]
    tm, k = xb_ref.shape
    w = ATT_GROUP_W
    if nres == 1:
        xb_ref[...] = x_ref[...].astype(BF16)
    else:
        for s in range(k // LANES):
            slab_ref[...] = x_ref[..., s * LANES:(s + 1) * LANES].reshape(tm, LANES)
            for r in range(nres):
                xb_ref[r * rows:(r + 1) * rows, s * LANES:(s + 1) * LANES] = (
                    slab_ref[pl.ds(r, rows, stride=nres), :].astype(BF16))

    for lo in range(0, 2 * w, PROJ_CHUNK):
        acc = jnp.dot(xb_ref[...], w_ref[:, lo:lo + PROJ_CHUNK], preferred_element_type=F32)
        if lo < w:
            acc = acc * QUERY_SCALE
        res = acc.astype(BF16)
        for r in range(nres):
            qv_ref[r, :, lo:lo + PROJ_CHUNK] = res[r * rows:(r + 1) * rows, :]

    for t0 in range(0, tm, PROJ_KT_TOKENS):
        acc_t = lax.dot_general(wkt_ref[...], xb_ref[t0:t0 + PROJ_KT_TOKENS, :],
                                (((1,), (1,)), ((), ())), preferred_element_type=F32).astype(BF16)
        for lo in range(0, PROJ_KT_TOKENS, BAND_BLOCK):
            r, c = (t0 + lo) // rows, ((t0 + lo) % rows) // BAND_BLOCK
            kt_ref[r, c] = acc_t[:, lo:lo + BAND_BLOCK]


def _proj_attn(x2d, w_qv, w_kt, layer, g, dil, batch, seq):
    m, k = x2d.shape
    w, blk, tm = ATT_GROUP_W, BAND_BLOCK, PROJ_TM
    length = seq // dil
    nres = min(dil, tm // blk)
    rows = tm // nres
    ngrp, nwin = dil // nres, length // rows
    assert dil % nres == 0 and length % rows == 0 and rows % blk == 0
    assert tm % PROJ_KT_TOKENS == 0 and PROJ_KT_TOKENS % blk == 0 and (2 * w) % PROJ_CHUNK == 0
    where = lambda i: (i // (nwin * ngrp), (i // ngrp) % nwin, i % ngrp)
    if ngrp == 1:
        xv = x2d
        x_spec = pl.BlockSpec((tm, k), lambda i: (i, 0))
    else:
        assert nres % 8 == 0
        xv = x2d.reshape(batch, nwin, rows, ngrp, nres, k)
        x_spec = pl.BlockSpec((None, None, rows, None, nres, k),
                              lambda i: (where(i)[0], where(i)[1], 0, where(i)[2], 0, 0))
    return pl.pallas_call(
        _proj_attn_kernel,
        grid=(m // tm,),
        in_specs=[x_spec, _resident(w_qv.shape[2:], (layer, g)), _resident(w_kt.shape[2:], (layer, g))],
        out_specs=[pl.BlockSpec((None, nres, rows, 2 * w),
                                lambda i: (where(i)[0], where(i)[2], where(i)[1], 0)),
                   pl.BlockSpec((None, nres, rows // blk, w, blk),
                                lambda i: (where(i)[0], where(i)[2], where(i)[1], 0, 0))],
        out_shape=[jax.ShapeDtypeStruct((batch, dil, length, 2 * w), BF16),
                   jax.ShapeDtypeStruct((batch, dil, length // blk, w, blk), BF16)],
        scratch_shapes=[pltpu.VMEM((tm, LANES), F32), pltpu.VMEM((tm, k), BF16)],
        compiler_params=_cparams(("parallel",)),
        name=f"proj_attn_d{dil}",
    )(x2d if ngrp == 1 else xv, w_qv, w_kt)


def _ret_kernel(q_ref, k_ref, qx_ref, kz_ref, v_ref, g_ref, intra_ref, o_ref, state_ref, *, gamma_c):
    @pl.when(pl.program_id(1) == 0)
    def _():
        state_ref[...] = jnp.zeros_like(state_ref)

    dk, dv, c = RET_QK_DIM, RET_V_DIM, RET_CHUNK
    for r0 in range(0, q_ref.shape[0], c):
        rows = slice(r0, r0 + c)
        for h in range(RET_HEADS):
            q = q_ref[rows, h * dk:(h + 1) * dk]
            k = k_ref[rows, h * dk:(h + 1) * dk]
            v = v_ref[rows, h * dv:(h + 1) * dv]
            s = lax.dot_general(q, k, (((1,), (1,)), ((), ())), preferred_element_type=F32)
            s = (s * intra_ref[h]).astype(BF16)
            st = state_ref[h]
            y = jnp.dot(jnp.concatenate([s, qx_ref[rows, h * dk:(h + 1) * dk]], axis=1),
                        jnp.concatenate([v, st.astype(BF16)], axis=0), preferred_element_type=F32)
            upd = lax.dot_general(kz_ref[rows, h * dk:(h + 1) * dk], v, (((0,), (0,)), ((), ())),
                                  preferred_element_type=F32)
            state_ref[h] = gamma_c[h] * st + upd
            mu = jnp.mean(y, axis=-1, keepdims=True)
            yc = y - mu
            var = jnp.mean(yc * yc, axis=-1, keepdims=True)
            yn = yc * lax.rsqrt(var + GN_EPS)
            gate = g_ref[rows, h * dv:(h + 1) * dv].astype(F32)
            o_ref[rows, h * dv:(h + 1) * dv] = (gate * yn).astype(BF16)


def _ret_tables(c):
    h = np.arange(RET_HEADS, dtype=np.float64)
    log_gamma = np.log(1.0 - 2.0 ** (-5.0 - h))
    idx = np.arange(c, dtype=np.float64)
    diff = idx[:, None] - idx[None, :]
    intra = np.where(diff[None] >= 0, np.exp(np.maximum(diff, 0.0)[None] * log_gamma[:, None, None]), 0.0)
    xi = np.exp((idx[None, :] + 1.0) * log_gamma[:, None])[..., None]
    zeta = np.exp((c - 1.0 - idx[None, :]) * log_gamma[:, None])[..., None]
    gamma_c = tuple(float(np.float32(v)) for v in np.exp(c * log_gamma))
    return (jnp.asarray(intra, F32), jnp.asarray(xi, F32), jnp.asarray(zeta, F32), gamma_c)


def _retention_core(p, batch, seq):
    c = RET_CHUNK * RET_CHUNKS_PER_STEP
    assert seq % c == 0
    nc = seq // c
    intra, _, _, gamma_c = _ret_tables(RET_CHUNK)
    row = lambda b, i: b * nc + i
    qk = lambda part: pl.BlockSpec((c, RET_QK), lambda b, i: (row(b, i), part))
    wide = lambda part: pl.BlockSpec((c, RET_V), lambda b, i: (row(b, i), part))
    return pl.pallas_call(
        functools.partial(_ret_kernel, gamma_c=gamma_c),
        grid=(batch, nc),
        in_specs=[qk(0), qk(1), qk(2), qk(3), wide(2), wide(3), _resident(intra.shape)],
        out_specs=wide(0),
        out_shape=jax.ShapeDtypeStruct((batch * seq, RET_V), BF16),
        scratch_shapes=[pltpu.VMEM((RET_HEADS, RET_QK_DIM, RET_V_DIM), F32)],
        compiler_params=_cparams(("parallel", "arbitrary")),
        name="retention_core",
    )(p, p, p, p, p, p, intra)


def _t5_bucket(dist):
    max_exact = NUM_BUCKETS // 2
    d_f = jnp.maximum(dist, 1).astype(jnp.float32)
    large = max_exact + (jnp.log(d_f / max_exact) / math.log(MAX_DISTANCE / max_exact)
                         * (NUM_BUCKETS - max_exact)).astype(jnp.int32)
    large = jnp.minimum(large, NUM_BUCKETS - 1)
    return jnp.where(dist < max_exact, dist, large)


def _band_buckets():
    blk = BAND_BLOCK
    qi = jnp.arange(blk)[:, None]
    kj = jnp.arange(2 * blk)[None, :]
    delta = qi + blk - kj
    out = []
    for window, dil in DILATED_PAIRS:
        steps = window // dil
        ok = (delta >= 0) & (delta <= steps)
        out.append(jnp.where(ok, _t5_bucket(jnp.maximum(delta, 0) * dil), -1))
    return jnp.stack(out).astype(jnp.int32)


def _bias_kernel(tab_ref, bkt_ref, o_ref):
    g = pl.program_id(0)
    bkt = bkt_ref[0]
    col = lax.broadcasted_iota(jnp.int32, bkt.shape, 1)

    def head(h, carry):
        acc = jnp.full(bkt.shape, MASK_VALUE, F32)
        for b in range(NUM_BUCKETS):
            acc = jnp.where(bkt == b, tab_ref[g * ATT_HEADS + h, b] * LOG2E, acc)
        o_ref[0, h] = acc
        o_ref[1, h] = jnp.where(col < BAND_BLOCK, MASK_VALUE, acc)
        return carry

    lax.fori_loop(0, ATT_HEADS, head, 0)


def _band_bias(rel_bias):
    gh = rel_bias.shape[0]
    blk = BAND_BLOCK
    return pl.pallas_call(
        _bias_kernel,
        grid=(gh // ATT_HEADS,),
        in_specs=[pl.BlockSpec(memory_space=pltpu.SMEM),
                  pl.BlockSpec((1, blk, 2 * blk), lambda i: (i, 0, 0))],
        out_specs=pl.BlockSpec((2, ATT_HEADS, blk, 2 * blk), lambda i: (0, i, 0, 0)),
        out_shape=jax.ShapeDtypeStruct((2, gh, blk, 2 * blk), F32),
        compiler_params=_cparams(("parallel",)),
        name="band_bias",
    )(rel_bias, _band_buckets())


def _stat_lane(h):
    return h // 2 + (ATT_HEAD_DIM if h % 2 == 0 else 0)


def _attn_kernel(q_ref, v_ref, vh_ref, kt_ref, kth_ref, bias_ref, o_ref, m_ref, d_ref, *, nblk):
    blk, hd = BAND_BLOCK, ATT_HEAD_DIM
    first_prog = pl.program_id(2) == 0
    lane = lax.broadcasted_iota(jnp.int32, (blk, LANES), 1)
    vlane = lax.broadcasted_iota(jnp.int32, (2 * blk, LANES), 1)
    zeros_k = jnp.zeros((hd, 2 * blk), BF16)
    ones_v = jnp.ones((2 * blk, LANES), BF16)

    def block(t, carry):
        r, n = lax.div(t, nblk), lax.rem(t, nblk)
        row = pl.multiple_of(n * blk, blk)
        pn = jnp.maximum(n - 1, 0)
        prow = pl.multiple_of(pn * blk, blk)
        first_blk = n == 0
        table = jnp.logical_and(first_blk, first_prog).astype(jnp.int32)
        m_tile = jnp.zeros((blk, LANES), F32)
        d_tile = jnp.ones((blk, LANES), F32)
        for hp in range(ATT_HEADS // 2):
            pair = slice(hp * LANES, (hp + 1) * LANES)
            q = q_ref[r, pl.ds(row, blk), pair]
            vp = jnp.where(first_blk, vh_ref[r, :, pair], v_ref[r, pl.ds(prow, blk), pair])
            vb = jnp.concatenate([vp, v_ref[r, pl.ds(row, blk), pair]], axis=0)
            parts = []
            for hh in range(2):
                h = 2 * hp + hh
                hrows = slice(h * hd, (h + 1) * hd)
                kp = jnp.where(first_blk, kth_ref[r, 0, hrows, :], kt_ref[r, pn, hrows, :])
                kbt = jnp.concatenate([kp, kt_ref[r, n, hrows, :]], axis=1)
                rhs = jnp.concatenate([kbt, zeros_k] if hh == 0 else [zeros_k, kbt], axis=0)
                s = jnp.dot(q, rhs, preferred_element_type=F32) + bias_ref[table, h]
                m = jnp.max(s, axis=-1, keepdims=True)
                e = jnp.exp2(s - m).astype(BF16)
                own = (vlane < hd) if hh == 0 else (vlane >= hd)
                part = jnp.dot(e, jnp.where(own, vb, ones_v), preferred_element_type=F32)
                stat = lane == _stat_lane(h)
                m_tile = jnp.where(stat, m, m_tile)
                d_tile = jnp.where(stat, part, d_tile)
                parts.append(part)
            o_ref[r, pl.ds(row, blk), pair] = jnp.where(lane < hd, parts[0], parts[1]).astype(BF16)
        m_ref[r, pl.ds(row, blk), :] = m_tile
        d_ref[r, pl.ds(row, blk), :] = d_tile
        return carry

    lax.fori_loop(0, q_ref.shape[0] * nblk, block, 0, unroll=ATT_UNROLL)


def _attn_group(qv, kt, bias, g):
    window, dil = DILATED_PAIRS[g]
    assert window // dil == BAND_BLOCK
    blk, w = BAND_BLOCK, ATT_GROUP_W
    batch, _, length, _ = qv.shape
    rows = min(length, ATT_ROWS_MAX)
    nres = min(dil, ATT_ROWS_MAX // rows)
    assert length % rows == 0 and rows % blk == 0 and dil % nres == 0
    nprog, nblk = length // rows, rows // blk
    main = lambda part: pl.BlockSpec((None, nres, rows, w), lambda b, r, s: (b, r, s, part))
    prev_blk = lambda s: jnp.maximum(s * nblk - 1, 0)
    return pl.pallas_call(
        functools.partial(_attn_kernel, nblk=nblk),
        grid=(batch, dil // nres, nprog),
        in_specs=[main(0), main(1),
                  pl.BlockSpec((None, nres, blk, w), lambda b, r, s: (b, r, prev_blk(s), 1)),
                  pl.BlockSpec((None, nres, nblk, w, blk), lambda b, r, s: (b, r, s, 0, 0)),
                  pl.BlockSpec((None, nres, 1, w, blk), lambda b, r, s: (b, r, prev_blk(s), 0, 0)),
                  pl.BlockSpec((2, ATT_HEADS, blk, 2 * blk), lambda b, r, s: (0, g, 0, 0),
                               pipeline_mode=pl.Buffered(1))],
        out_specs=[pl.BlockSpec((None, nres, rows, w), lambda b, r, s: (b, r, s, 0)),
                   pl.BlockSpec((None, nres, rows, LANES), lambda b, r, s: (b, r, s, 0)),
                   pl.BlockSpec((None, nres, rows, LANES), lambda b, r, s: (b, r, s, 0))],
        out_shape=[jax.ShapeDtypeStruct((batch, dil, length, w), BF16),
                   jax.ShapeDtypeStruct((batch, dil, length, LANES), F32),
                   jax.ShapeDtypeStruct((batch, dil, length, LANES), F32)],
        compiler_params=_cparams(("parallel", "parallel", "arbitrary")),
        name=f"attn_g{g}",
    )(qv, qv, qv, kt, kt, bias)


def _layer_norm(v, gain, bias):
    mu = jnp.mean(v, axis=-1, keepdims=True)
    vc = v - mu
    var = jnp.mean(vc * vc, axis=-1, keepdims=True)
    return vc * lax.rsqrt(var + LN_EPS) * gain + bias


def _token_order(src_ref, dst_ref):
    dil, rows, width = src_ref.shape
    for r in range(dil):
        for s in range(width // LANES):
            dst_ref[s, pl.ds(r, rows, stride=dil), :] = (
                src_ref[r, :, s * LANES:(s + 1) * LANES].astype(F32))


def _merge_groups(o_refs, m_refs, d_refs, ex_ref, scratch):
    nums, maxs, dens = [], [], []
    for o_ref, m_ref, d_ref in zip(o_refs, m_refs, d_refs):
        if o_ref.shape[0] == 1:
            nums.append(o_ref[0].astype(F32))
            maxs.append(m_ref[0])
            dens.append(d_ref[0])
        else:
            so_ref, sm_ref, sd_ref = (scratch.pop(0) for _ in range(3))
            _token_order(o_ref, so_ref)
            _token_order(m_ref, sm_ref)
            _token_order(d_ref, sd_ref)
            nums.append(jnp.concatenate([so_ref[s] for s in range(so_ref.shape[0])], axis=1))
            maxs.append(sm_ref[0])
            dens.append(sd_ref[0])
    mx = jnp.maximum(jnp.maximum(maxs[0], maxs[1]), maxs[2])
    es = [jnp.exp2(m - mx) for m in maxs]
    inv = 1.0 / (es[0] * dens[0] + es[1] * dens[1] + es[2] * dens[2])
    z = None
    for e, num in zip(es, nums):
        wgt = (e * inv).astype(BF16)
        term = jnp.dot(wgt, ex_ref[...], preferred_element_type=F32) * num
        z = term if z is None else z + term
    return z.astype(BF16)


def _tail_kernel(*refs, merge, alpha):
    if merge:
        g = ATT_GROUPS
        o_refs, m_refs, d_refs = refs[:g], refs[g:2 * g], refs[2 * g:3 * g]
        ex_ref, x_ref, wo_ref, wu_ref, wd_ref, gain_ref, bias_ref, out_ref, *scratch = refs[3 * g:]
    else:
        z_ref, x_ref, wo_ref, wu_ref, wd_ref, gain_ref, bias_ref, out_ref = refs
    z = _merge_groups(o_refs, m_refs, d_refs, ex_ref, scratch) if merge else z_ref[...]
    mix = jnp.dot(z, wo_ref[...], preferred_element_type=F32)
    x1 = _layer_norm(alpha * x_ref[...] + mix, gain_ref[0:1, :], bias_ref[0:1, :])
    x1b = x1.astype(BF16)
    acc = jnp.zeros(x1.shape, F32)
    for c0 in range(0, D_FF, FF_CHUNK):
        gate = jnp.dot(x1b, wu_ref[:, c0:c0 + FF_CHUNK], preferred_element_type=F32)
        up = jnp.dot(x1b, wu_ref[:, D_FF + c0:D_FF + c0 + FF_CHUNK], preferred_element_type=F32)
        hid = gate * (1.0 / (1.0 + jnp.exp(-gate))) * up
        acc = acc + jnp.dot(hid.astype(BF16), wd_ref[c0:c0 + FF_CHUNK, :], preferred_element_type=F32)
    out_ref[...] = _layer_norm(alpha * x1 + acc, gain_ref[1:2, :], bias_ref[1:2, :])


def _tail(mix_in, x2d, w_out, mixer_layer, w_up, w_down, ln_gain, ln_bias, layer, alpha):
    m, d = x2d.shape
    tm = TAIL_TM
    assert m % tm == 0 and D_FF % FF_CHUNK == 0
    merge = isinstance(mix_in, tuple)
    row = lambda width: pl.BlockSpec((tm, width), lambda i: (i, 0))
    scratch = []
    if merge:
        expand = np.zeros((LANES, ATT_GROUP_W), np.float32)
        for h in range(ATT_HEADS):
            expand[_stat_lane(h), h * ATT_HEAD_DIM:(h + 1) * ATT_HEAD_DIM] = 1.0
        expand = jnp.asarray(expand, BF16)
        in_specs = []
        for a in mix_in:
            batch, dil, length, width = a.shape
            nseq = dil * length // tm
            assert tm % (dil * 16) == 0
            in_specs.append(pl.BlockSpec((None, dil, tm // dil, width),
                                         lambda i, nseq=nseq: (i // nseq, 0, i % nseq, 0)))
        for a in mix_in[:ATT_GROUPS]:
            if a.shape[1] > 1:
                scratch += [pltpu.VMEM((ATT_GROUP_W // LANES, tm, LANES), F32),
                            pltpu.VMEM((1, tm, LANES), F32), pltpu.VMEM((1, tm, LANES), F32)]
        args = list(mix_in) + [expand]
        in_specs.append(_resident(expand.shape))
    else:
        args = [mix_in]
        in_specs = [row(mix_in.shape[1])]
    args += [x2d, w_out, w_up, w_down, ln_gain, ln_bias]
    in_specs += [row(d), _resident(w_out.shape[1:], (mixer_layer,)),
                 _resident(w_up.shape[1:], (layer,)), _resident(w_down.shape[1:], (layer,)),
                 _resident(ln_gain.shape[1:], (layer,)), _resident(ln_bias.shape[1:], (layer,))]
    return pl.pallas_call(
        functools.partial(_tail_kernel, merge=merge, alpha=alpha),
        grid=(m // tm,),
        in_specs=in_specs,
        out_specs=row(d),
        out_shape=jax.ShapeDtypeStruct((m, d), F32),
        scratch_shapes=scratch,
        compiler_params=_cparams(("parallel",)),
        name="tail_attn" if merge else "tail_ret",
    )(*args)


def _rope_tables(seq):
    d = RET_QK_DIM
    inv = 1.0 / (ROPE_BASE ** jnp.linspace(0.0, 1.0, d // 2, dtype=jnp.float32))
    ang = jnp.arange(seq, dtype=jnp.float32)[:, None] * inv[None, :]
    return jnp.cos(ang), jnp.sin(ang)


def kernel(x, ret_w_in, ret_w_out, attn_w_in, attn_w_out, rel_bias, ffn_w_up, ffn_w_down, ln_gain, ln_bias):
    batch, seq, d = x.shape
    depth = ffn_w_up.shape[0]
    alpha = (2 * depth) ** 0.25
    cos, sin = _rope_tables(seq)
    bias = _band_bias(rel_bias)
    ret_in, ret_out = _ret_w_in_prep(ret_w_in), _cast_bf16(ret_w_out)
    att_qv, att_kt = _attn_w_in_prep(attn_w_in)
    att_out = _cast_bf16(attn_w_out)
    w_up, w_down = _cast_bf16(ffn_w_up), _cast_bf16(ffn_w_down)
    h = x.reshape(batch * seq, d)
    for i in range(depth):
        j = i // 2
        if i % 2 == 0:
            p = _proj_rope(h, ret_in, j, cos, sin, seq)
            mix_in, w_out = _retention_core(p, batch, seq), ret_out
        else:
            outs = []
            for g, (_, dil) in enumerate(DILATED_PAIRS):
                qv, kt = _proj_attn(h, att_qv, att_kt, j, g, dil, batch, seq)
                outs.append(_attn_group(qv, kt, bias, g))
            mix_in = tuple(a[0] for a in outs) + tuple(a[1] for a in outs) + tuple(a[2] for a in outs)
            w_out = att_out
        h = _tail(mix_in, h, w_out, j, w_up, w_down, ln_gain, ln_bias, i, alpha)
    return h.reshape(batch, seq, d)
```

```python
import functools
import math

import numpy as np
import jax
import jax.numpy as jnp
from jax import lax
from jax.experimental import pallas as pl
from jax.experimental.pallas import tpu as pltpu

F32 = jnp.float32
BF16 = jnp.bfloat16

D_MODEL = 1024
RET_QK_DIM = 256
RET_HEADS = D_MODEL // RET_QK_DIM
RET_V_DIM = 2 * RET_QK_DIM
RET_QK = RET_HEADS * RET_QK_DIM
RET_V = RET_HEADS * RET_V_DIM
ROPE_BASE = 10000.0
DILATED_PAIRS = ((128, 1), (512, 4), (2048, 16))
ATT_GROUPS = 3
ATT_HEAD_DIM = 64
ATT_HEADS = D_MODEL // ATT_HEAD_DIM
ATT_GROUP_W = ATT_HEADS * ATT_HEAD_DIM
BAND_BLOCK = 128
NUM_BUCKETS = 32
MAX_DISTANCE = 2048
D_FF = ((8 * D_MODEL + 3 * 256 - 1) // (3 * 256)) * 256
LN_EPS = 1e-5
GN_EPS = 1e-5

LANES = 128
VMEM_LIMIT_BYTES = 56 * 1024 * 1024
RET_CHUNK = 256
RET_CHUNKS_PER_STEP = 2
PROJ_TM = 1024
PROJ_ROPE_TM = 512
PROJ_CHUNK = 512
PROJ_KT_TOKENS = 256
ATT_ROWS_MAX = 1024
ATT_UNROLL = 8
TAIL_TM = 512
CAST_BLOCK_BYTES = 12 * 1024 * 1024
FF_CHUNK = 256
MASK_VALUE = -1e30
LOG2E = math.log2(math.e)
QUERY_SCALE = ATT_HEAD_DIM ** -0.5 * LOG2E


def _cparams(sem):
    return pltpu.CompilerParams(dimension_semantics=sem, vmem_limit_bytes=VMEM_LIMIT_BYTES)


def _resident(shape, lead=()):
    nd = len(shape)
    return pl.BlockSpec((None,) * len(lead) + tuple(shape), lambda *_: tuple(lead) + (0,) * nd,
                        pipeline_mode=pl.Buffered(1))


def _cast_kernel(w_ref, o_ref):
    o_ref[...] = w_ref[...].astype(BF16)


def _cast_rows(r, c):
    fits = [t for t in range(16, r + 1, 16) if r % t == 0 and t * c * 4 <= CAST_BLOCK_BYTES]
    assert fits
    return fits[-1]


def _cast_bf16(w):
    layers, r, c = w.shape
    tr = _cast_rows(r, c)
    spec = pl.BlockSpec((None, tr, c), lambda l, i: (l, i, 0))
    return pl.pallas_call(
        _cast_kernel, grid=(layers, r // tr), in_specs=[spec], out_specs=spec,
        out_shape=jax.ShapeDtypeStruct(w.shape, BF16),
        compiler_params=_cparams(("parallel", "parallel")), name="cast_bf16",
    )(w)


def _ret_w_in_kernel(w_ref, perm_ref, o_ref):
    wb = w_ref[...].astype(BF16)
    for h in range(2 * RET_HEADS):
        cols = slice(h * RET_QK_DIM, (h + 1) * RET_QK_DIM)
        o_ref[:, cols] = jnp.dot(wb[:, cols], perm_ref[...], preferred_element_type=F32).astype(BF16)
    o_ref[:, 2 * RET_QK:] = wb[:, 2 * RET_QK:]


def _ret_w_in_prep(w):
    layers, r, c = w.shape
    tr = _cast_rows(r, c)
    d = RET_QK_DIM
    perm = np.zeros((d, d), np.float32)
    perm[2 * np.arange(d // 2), np.arange(d // 2)] = 1.0
    perm[2 * np.arange(d // 2) + 1, d // 2 + np.arange(d // 2)] = 1.0
    spec = pl.BlockSpec((None, tr, c), lambda l, i: (l, i, 0))
    return pl.pallas_call(
        _ret_w_in_kernel, grid=(layers, r // tr),
        in_specs=[spec, pl.BlockSpec((d, d), lambda l, i: (0, 0))], out_specs=spec,
        out_shape=jax.ShapeDtypeStruct(w.shape, BF16),
        compiler_params=_cparams(("parallel", "parallel")), name="ret_w_in_prep",
    )(w, jnp.asarray(perm, BF16))


def _attn_w_in_kernel(q_ref, k_ref, v_ref, qv_ref, kt_ref):
    w = ATT_GROUP_W
    qv_ref[:, :w] = q_ref[...].astype(BF16)
    qv_ref[:, w:] = v_ref[...].astype(BF16)
    kt_ref[...] = k_ref[...].T.astype(BF16)


def _attn_w_in_prep(w):
    layers, d, _ = w.shape
    g, gw = ATT_GROUPS, ATT_GROUP_W
    part = lambda p: pl.BlockSpec((None, d, gw), lambda l, i: (l, 0, p * g + i))
    return pl.pallas_call(
        _attn_w_in_kernel, grid=(layers, g),
        in_specs=[part(0), part(1), part(2)],
        out_specs=[pl.BlockSpec((None, None, d, 2 * gw), lambda l, i: (l, i, 0, 0)),
                   pl.BlockSpec((None, None, gw, d), lambda l, i: (l, i, 0, 0))],
        out_shape=[jax.ShapeDtypeStruct((layers, g, d, 2 * gw), BF16),
                   jax.ShapeDtypeStruct((layers, g, gw, d), BF16)],
        compiler_params=_cparams(("parallel", "parallel")), name="attn_w_in_prep",
    )(w, w, w)


def _proj_rope_kernel(x_ref, w_ref, cos_ref, sin_ref, o_ref):
    xb = x_ref[...].astype(BF16)
    n = w_ref.shape[1]
    half = RET_QK_DIM // 2
    for lo in range(2 * RET_QK + RET_V, n, PROJ_CHUNK):
        acc = jnp.dot(xb, w_ref[:, lo:lo + PROJ_CHUNK], preferred_element_type=F32)
        o_ref[:, lo:lo + PROJ_CHUNK] = (acc * (0.5 * jnp.tanh(0.5 * acc) + 0.5)).astype(BF16)
    for part in range(2):
        scale = 1.0 if part == 0 else RET_QK_DIM ** -0.5
        c = cos_ref[...] * scale
        s = sin_ref[...] * scale
        for h in range(RET_HEADS):
            lo = part * RET_QK + h * RET_QK_DIM
            acc = jnp.dot(xb, w_ref[:, lo:lo + RET_QK_DIM], preferred_element_type=F32)
            a, b = acc[:, :half], acc[:, half:]
            o_ref[:, lo:lo + half] = (a * c - b * s).astype(BF16)
            o_ref[:, lo + half:lo + 2 * half] = (a * s + b * c).astype(BF16)
    for lo in range(2 * RET_QK, 2 * RET_QK + RET_V, PROJ_CHUNK):
        o_ref[:, lo:lo + PROJ_CHUNK] = jnp.dot(
            xb, w_ref[:, lo:lo + PROJ_CHUNK], preferred_element_type=F32).astype(BF16)


def _proj_rope(x2d, w, layer, cos, sin, seq):
    m, k = x2d.shape
    n = w.shape[2]
    tm = PROJ_ROPE_TM
    assert m % tm == 0 and seq % tm == 0 and RET_V % PROJ_CHUNK == 0
    nseq = seq // tm
    table = pl.BlockSpec((tm, RET_QK_DIM // 2), lambda i: (i % nseq, 0))
    return pl.pallas_call(
        _proj_rope_kernel,
        grid=(m // tm,),
        in_specs=[pl.BlockSpec((tm, k), lambda i: (i, 0)), _resident(w.shape[1:], (layer,)),
                  table, table],
        out_specs=pl.BlockSpec((tm, n), lambda i: (i, 0)),
        out_shape=jax.ShapeDtypeStruct((m, n), BF16),
        compiler_params=_cparams(("parallel",)),
        name="proj_rope",
    )(x2d, w, cos, sin)


def _proj_attn_kernel(x_ref, w_ref, wkt_ref, qv_ref, kt_ref, slab_ref, xb_ref):
    nres, rows = qv_ref.shape[0], qv_ref.shape[1]
    tm, k = xb_ref.shape
    w = ATT_GROUP_W
    if nres == 1:
        xb_ref[...] = x_ref[...].astype(BF16)
    else:
        for s in range(k // LANES):
            slab_ref[...] = x_ref[..., s * LANES:(s + 1) * LANES].reshape(tm, LANES)
            for r in range(nres):
                xb_ref[r * rows:(r + 1) * rows, s * LANES:(s + 1) * LANES] = (
                    slab_ref[pl.ds(r, rows, stride=nres), :].astype(BF16))

    for lo in range(0, 2 * w, PROJ_CHUNK):
        acc = jnp.dot(xb_ref[...], w_ref[:, lo:lo + PROJ_CHUNK], preferred_element_type=F32)
        if lo < w:
            acc = acc * QUERY_SCALE
        res = acc.astype(BF16)
        for r in range(nres):
            qv_ref[r, :, lo:lo + PROJ_CHUNK] = res[r * rows:(r + 1) * rows, :]

    for t0 in range(0, tm, PROJ_KT_TOKENS):
        acc_t = lax.dot_general(wkt_ref[...], xb_ref[t0:t0 + PROJ_KT_TOKENS, :],
                                (((1,), (1,)), ((), ())), preferred_element_type=F32).astype(BF16)
        for lo in range(0, PROJ_KT_TOKENS, BAND_BLOCK):
            r, c = (t0 + lo) // rows, ((t0 + lo) % rows) // BAND_BLOCK
            kt_ref[r, c] = acc_t[:, lo:lo + BAND_BLOCK]


def _proj_attn(x2d, w_qv, w_kt, layer, g, dil, batch, seq):
    m, k = x2d.shape
    w, blk, tm = ATT_GROUP_W, BAND_BLOCK, PROJ_TM
    length = seq // dil
    nres = min(dil, tm // blk)
    rows = tm // nres
    ngrp, nwin = dil // nres, length // rows
    assert dil % nres == 0 and length % rows == 0 and rows % blk == 0
    assert tm % PROJ_KT_TOKENS == 0 and PROJ_KT_TOKENS % blk == 0 and (2 * w) % PROJ_CHUNK == 0
    where = lambda i: (i // (nwin * ngrp), (i // ngrp) % nwin, i % ngrp)
    if ngrp == 1:
        xv = x2d
        x_spec = pl.BlockSpec((tm, k), lambda i: (i, 0))
    else:
        assert nres % 8 == 0
        xv = x2d.reshape(batch, nwin, rows, ngrp, nres, k)
        x_spec = pl.BlockSpec((None, None, rows, None, nres, k),
                              lambda i: (where(i)[0], where(i)[1], 0, where(i)[2], 0, 0))
    return pl.pallas_call(
        _proj_attn_kernel,
        grid=(m // tm,),
        in_specs=[x_spec, _resident(w_qv.shape[2:], (layer, g)), _resident(w_kt.shape[2:], (layer, g))],
        out_specs=[pl.BlockSpec((None, nres, rows, 2 * w),
                                lambda i: (where(i)[0], where(i)[2], where(i)[1], 0)),
                   pl.BlockSpec((None, nres, rows // blk, w, blk),
                                lambda i: (where(i)[0], where(i)[2], where(i)[1], 0, 0))],
        out_shape=[jax.ShapeDtypeStruct((batch, dil, length, 2 * w), BF16),
                   jax.ShapeDtypeStruct((batch, dil, length // blk, w, blk), BF16)],
        scratch_shapes=[pltpu.VMEM((tm, LANES), F32), pltpu.VMEM((tm, k), BF16)],
        compiler_params=_cparams(("parallel",)),
        name=f"proj_attn_d{dil}",
    )(x2d if ngrp == 1 else xv, w_qv, w_kt)


def _ret_kernel(q_ref, k_ref, v_ref, g_ref, intra_ref, xi_ref, zeta_ref, o_ref, state_ref, *, gamma_c):
    @pl.when(pl.program_id(1) == 0)
    def _():
        state_ref[...] = jnp.zeros_like(state_ref)

    dk, dv, c = RET_QK_DIM, RET_V_DIM, RET_CHUNK
    for r0 in range(0, q_ref.shape[0], c):
        rows = slice(r0, r0 + c)
        for h in range(RET_HEADS):
            q = q_ref[rows, h * dk:(h + 1) * dk]
            k = k_ref[rows, h * dk:(h + 1) * dk]
            v = v_ref[rows, h * dv:(h + 1) * dv]
            s = lax.dot_general(q, k, (((1,), (1,)), ((), ())), preferred_element_type=F32)
            s = (s * intra_ref[h]).astype(BF16)
            st = state_ref[h]
            qx = (q.astype(F32) * xi_ref[h]).astype(BF16)
            y = jnp.dot(jnp.concatenate([s, qx], axis=1),
                        jnp.concatenate([v, st.astype(BF16)], axis=0), preferred_element_type=F32)
            kz = (k.astype(F32) * zeta_ref[h]).astype(BF16)
            upd = lax.dot_general(kz, v, (((0,), (0,)), ((), ())), preferred_element_type=F32)
            state_ref[h] = gamma_c[h] * st + upd
            mu = jnp.mean(y, axis=-1, keepdims=True)
            yc = y - mu
            var = jnp.mean(yc * yc, axis=-1, keepdims=True)
            yn = yc * lax.rsqrt(var + GN_EPS)
            gate = g_ref[rows, h * dv:(h + 1) * dv].astype(F32)
            o_ref[rows, h * dv:(h + 1) * dv] = (gate * yn).astype(BF16)


def _ret_tables(c):
    h = np.arange(RET_HEADS, dtype=np.float64)
    log_gamma = np.log(1.0 - 2.0 ** (-5.0 - h))
    idx = np.arange(c, dtype=np.float64)
    diff = idx[:, None] - idx[None, :]
    intra = np.where(diff[None] >= 0, np.exp(np.maximum(diff, 0.0)[None] * log_gamma[:, None, None]), 0.0)
    xi = np.exp((idx[None, :] + 1.0) * log_gamma[:, None])[..., None]
    zeta = np.exp((c - 1.0 - idx[None, :]) * log_gamma[:, None])[..., None]
    gamma_c = tuple(float(np.float32(v)) for v in np.exp(c * log_gamma))
    return (jnp.asarray(intra, F32), jnp.asarray(xi, F32), jnp.asarray(zeta, F32), gamma_c)


def _retention_core(p, batch, seq):
    c = RET_CHUNK * RET_CHUNKS_PER_STEP
    assert seq % c == 0
    nc = seq // c
    intra, xi, zeta, gamma_c = _ret_tables(RET_CHUNK)
    row = lambda b, i: b * nc + i
    qk = lambda part: pl.BlockSpec((c, RET_QK), lambda b, i: (row(b, i), part))
    wide = lambda part: pl.BlockSpec((c, RET_V), lambda b, i: (row(b, i), part))
    return pl.pallas_call(
        functools.partial(_ret_kernel, gamma_c=gamma_c),
        grid=(batch, nc),
        in_specs=[qk(0), qk(1), wide(1), wide(2),
                  _resident(intra.shape), _resident(xi.shape), _resident(zeta.shape)],
        out_specs=wide(0),
        out_shape=jax.ShapeDtypeStruct((batch * seq, RET_V), BF16),
        scratch_shapes=[pltpu.VMEM((RET_HEADS, RET_QK_DIM, RET_V_DIM), F32)],
        compiler_params=_cparams(("parallel", "arbitrary")),
        name="retention_core",
    )(p, p, p, p, intra, xi, zeta)


def _t5_bucket(dist):
    max_exact = NUM_BUCKETS // 2
    d_f = jnp.maximum(dist, 1).astype(jnp.float32)
    large = max_exact + (jnp.log(d_f / max_exact) / math.log(MAX_DISTANCE / max_exact)
                         * (NUM_BUCKETS - max_exact)).astype(jnp.int32)
    large = jnp.minimum(large, NUM_BUCKETS - 1)
    return jnp.where(dist < max_exact, dist, large)


def _band_buckets():
    blk = BAND_BLOCK
    qi = jnp.arange(blk)[:, None]
    kj = jnp.arange(2 * blk)[None, :]
    delta = qi + blk - kj
    out = []
    for window, dil in DILATED_PAIRS:
        steps = window // dil
        ok = (delta >= 0) & (delta <= steps)
        out.append(jnp.where(ok, _t5_bucket(jnp.maximum(delta, 0) * dil), -1))
    return jnp.stack(out).astype(jnp.int32)


def _bias_kernel(tab_ref, bkt_ref, o_ref):
    g = pl.program_id(0)
    bkt = bkt_ref[0]
    col = lax.broadcasted_iota(jnp.int32, bkt.shape, 1)

    def head(h, carry):
        acc = jnp.full(bkt.shape, MASK_VALUE, F32)
        for b in range(NUM_BUCKETS):
            acc = jnp.where(bkt == b, tab_ref[g * ATT_HEADS + h, b] * LOG2E, acc)
        o_ref[0, h] = acc
        o_ref[1, h] = jnp.where(col < BAND_BLOCK, MASK_VALUE, acc)
        return carry

    lax.fori_loop(0, ATT_HEADS, head, 0)


def _band_bias(rel_bias):
    gh = rel_bias.shape[0]
    blk = BAND_BLOCK
    return pl.pallas_call(
        _bias_kernel,
        grid=(gh // ATT_HEADS,),
        in_specs=[pl.BlockSpec(memory_space=pltpu.SMEM),
                  pl.BlockSpec((1, blk, 2 * blk), lambda i: (i, 0, 0))],
        out_specs=pl.BlockSpec((2, ATT_HEADS, blk, 2 * blk), lambda i: (0, i, 0, 0)),
        out_shape=jax.ShapeDtypeStruct((2, gh, blk, 2 * blk), F32),
        compiler_params=_cparams(("parallel",)),
        name="band_bias",
    )(rel_bias, _band_buckets())


def _stat_lane(h):
    return h // 2 + (ATT_HEAD_DIM if h % 2 == 0 else 0)


def _attn_kernel(q_ref, v_ref, vh_ref, kt_ref, kth_ref, bias_ref, o_ref, m_ref, d_ref, *, nblk):
    blk, hd = BAND_BLOCK, ATT_HEAD_DIM
    first_prog = pl.program_id(2) == 0
    lane = lax.broadcasted_iota(jnp.int32, (blk, LANES), 1)
    vlane = lax.broadcasted_iota(jnp.int32, (2 * blk, LANES), 1)
    zeros_k = jnp.zeros((hd, 2 * blk), BF16)
    ones_v = jnp.ones((2 * blk, LANES), BF16)

    def block(t, carry):
        r, n = lax.div(t, nblk), lax.rem(t, nblk)
        row = pl.multiple_of(n * blk, blk)
        pn = jnp.maximum(n - 1, 0)
        prow = pl.multiple_of(pn * blk, blk)
        first_blk = n == 0
        table = jnp.logical_and(first_blk, first_prog).astype(jnp.int32)
        m_tile = jnp.zeros((blk, LANES), F32)
        d_tile = jnp.ones((blk, LANES), F32)
        for hp in range(ATT_HEADS // 2):
            pair = slice(hp * LANES, (hp + 1) * LANES)
            q = q_ref[r, pl.ds(row, blk), pair]
            vp = jnp.where(first_blk, vh_ref[r, :, pair], v_ref[r, pl.ds(prow, blk), pair])
            vb = jnp.concatenate([vp, v_ref[r, pl.ds(row, blk), pair]], axis=0)
            parts = []
            for hh in range(2):
                h = 2 * hp + hh
                hrows = slice(h * hd, (h + 1) * hd)
                kp = jnp.where(first_blk, kth_ref[r, 0, hrows, :], kt_ref[r, pn, hrows, :])
                kbt = jnp.concatenate([kp, kt_ref[r, n, hrows, :]], axis=1)
                rhs = jnp.concatenate([kbt, zeros_k] if hh == 0 else [zeros_k, kbt], axis=0)
                s = jnp.dot(q, rhs, preferred_element_type=F32) + bias_ref[table, h]
                m = jnp.max(s, axis=-1, keepdims=True)
                e = jnp.exp2(s - m).astype(BF16)
                own = (vlane < hd) if hh == 0 else (vlane >= hd)
                part = jnp.dot(e, jnp.where(own, vb, ones_v), preferred_element_type=F32)
                stat = lane == _stat_lane(h)
                m_tile = jnp.where(stat, m, m_tile)
                d_tile = jnp.where(stat, part, d_tile)
                parts.append(part)
            o_ref[r, pl.ds(row, blk), pair] = jnp.where(lane < hd, parts[0], parts[1]).astype(BF16)
        m_ref[r, pl.ds(row, blk), :] = m_tile
        d_ref[r, pl.ds(row, blk), :] = d_tile
        return carry

    lax.fori_loop(0, q_ref.shape[0] * nblk, block, 0, unroll=ATT_UNROLL)


def _attn_group(qv, kt, bias, g):
    window, dil = DILATED_PAIRS[g]
    assert window // dil == BAND_BLOCK
    blk, w = BAND_BLOCK, ATT_GROUP_W
    batch, _, length, _ = qv.shape
    rows = min(length, ATT_ROWS_MAX)
    nres = min(dil, ATT_ROWS_MAX // rows)
    assert length % rows == 0 and rows % blk == 0 and dil % nres == 0
    nprog, nblk = length // rows, rows // blk
    main = lambda part: pl.BlockSpec((None, nres, rows, w), lambda b, r, s: (b, r, s, part))
    prev_blk = lambda s: jnp.maximum(s * nblk - 1, 0)
    return pl.pallas_call(
        functools.partial(_attn_kernel, nblk=nblk),
        grid=(batch, dil // nres, nprog),
        in_specs=[main(0), main(1),
                  pl.BlockSpec((None, nres, blk, w), lambda b, r, s: (b, r, prev_blk(s), 1)),
                  pl.BlockSpec((None, nres, nblk, w, blk), lambda b, r, s: (b, r, s, 0, 0)),
                  pl.BlockSpec((None, nres, 1, w, blk), lambda b, r, s: (b, r, prev_blk(s), 0, 0)),
                  pl.BlockSpec((2, ATT_HEADS, blk, 2 * blk), lambda b, r, s: (0, g, 0, 0),
                               pipeline_mode=pl.Buffered(1))],
        out_specs=[pl.BlockSpec((None, nres, rows, w), lambda b, r, s: (b, r, s, 0)),
                   pl.BlockSpec((None, nres, rows, LANES), lambda b, r, s: (b, r, s, 0)),
                   pl.BlockSpec((None, nres, rows, LANES), lambda b, r, s: (b, r, s, 0))],
        out_shape=[jax.ShapeDtypeStruct((batch, dil, length, w), BF16),
                   jax.ShapeDtypeStruct((batch, dil, length, LANES), F32),
                   jax.ShapeDtypeStruct((batch, dil, length, LANES), F32)],
        compiler_params=_cparams(("parallel", "parallel", "arbitrary")),
        name=f"attn_g{g}",
    )(qv, qv, qv, kt, kt, bias)


def _layer_norm(v, gain, bias):
    mu = jnp.mean(v, axis=-1, keepdims=True)
    vc = v - mu
    var = jnp.mean(vc * vc, axis=-1, keepdims=True)
    return vc * lax.rsqrt(var + LN_EPS) * gain + bias


def _token_order(src_ref, dst_ref):
    dil, rows, width = src_ref.shape
    for r in range(dil):
        for s in range(width // LANES):
            dst_ref[s, pl.ds(r, rows, stride=dil), :] = (
                src_ref[r, :, s * LANES:(s + 1) * LANES].astype(F32))


def _merge_groups(o_refs, m_refs, d_refs, ex_ref, scratch):
    nums, maxs, dens = [], [], []
    for o_ref, m_ref, d_ref in zip(o_refs, m_refs, d_refs):
        if o_ref.shape[0] == 1:
            nums.append(o_ref[0].astype(F32))
            maxs.append(m_ref[0])
            dens.append(d_ref[0])
        else:
            so_ref, sm_ref, sd_ref = (scratch.pop(0) for _ in range(3))
            _token_order(o_ref, so_ref)
            _token_order(m_ref, sm_ref)
            _token_order(d_ref, sd_ref)
            nums.append(jnp.concatenate([so_ref[s] for s in range(so_ref.shape[0])], axis=1))
            maxs.append(sm_ref[0])
            dens.append(sd_ref[0])
    mx = jnp.maximum(jnp.maximum(maxs[0], maxs[1]), maxs[2])
    es = [jnp.exp2(m - mx) for m in maxs]
    inv = 1.0 / (es[0] * dens[0] + es[1] * dens[1] + es[2] * dens[2])
    z = None
    for e, num in zip(es, nums):
        wgt = (e * inv).astype(BF16)
        term = jnp.dot(wgt, ex_ref[...], preferred_element_type=F32) * num
        z = term if z is None else z + term
    return z.astype(BF16)


def _tail_kernel(*refs, merge, alpha):
    if merge:
        g = ATT_GROUPS
        o_refs, m_refs, d_refs = refs[:g], refs[g:2 * g], refs[2 * g:3 * g]
        ex_ref, x_ref, wo_ref, wu_ref, wd_ref, gain_ref, bias_ref, out_ref, *scratch = refs[3 * g:]
    else:
        z_ref, x_ref, wo_ref, wu_ref, wd_ref, gain_ref, bias_ref, out_ref = refs
    z = _merge_groups(o_refs, m_refs, d_refs, ex_ref, scratch) if merge else z_ref[...]
    mix = jnp.dot(z, wo_ref[...], preferred_element_type=F32)
    x1 = _layer_norm(alpha * x_ref[...] + mix, gain_ref[0:1, :], bias_ref[0:1, :])
    x1b = x1.astype(BF16)
    acc = jnp.zeros(x1.shape, F32)
    for c0 in range(0, D_FF, FF_CHUNK):
        gate = jnp.dot(x1b, wu_ref[:, c0:c0 + FF_CHUNK], preferred_element_type=F32)
        up = jnp.dot(x1b, wu_ref[:, D_FF + c0:D_FF + c0 + FF_CHUNK], preferred_element_type=F32)
        hid = gate * (1.0 / (1.0 + jnp.exp(-gate))) * up
        acc = acc + jnp.dot(hid.astype(BF16), wd_ref[c0:c0 + FF_CHUNK, :], preferred_element_type=F32)
    out_ref[...] = _layer_norm(alpha * x1 + acc, gain_ref[1:2, :], bias_ref[1:2, :])


def _tail(mix_in, x2d, w_out, mixer_layer, w_up, w_down, ln_gain, ln_bias, layer, alpha):
    m, d = x2d.shape
    tm = TAIL_TM
    assert m % tm == 0 and D_FF % FF_CHUNK == 0
    merge = isinstance(mix_in, tuple)
    row = lambda width: pl.BlockSpec((tm, width), lambda i: (i, 0))
    scratch = []
    if merge:
        expand = np.zeros((LANES, ATT_GROUP_W), np.float32)
        for h in range(ATT_HEADS):
            expand[_stat_lane(h), h * ATT_HEAD_DIM:(h + 1) * ATT_HEAD_DIM] = 1.0
        expand = jnp.asarray(expand, BF16)
        in_specs = []
        for a in mix_in:
            batch, dil, length, width = a.shape
            nseq = dil * length // tm
            assert tm % (dil * 16) == 0
            in_specs.append(pl.BlockSpec((None, dil, tm // dil, width),
                                         lambda i, nseq=nseq: (i // nseq, 0, i % nseq, 0)))
        for a in mix_in[:ATT_GROUPS]:
            if a.shape[1] > 1:
                scratch += [pltpu.VMEM((ATT_GROUP_W // LANES, tm, LANES), F32),
                            pltpu.VMEM((1, tm, LANES), F32), pltpu.VMEM((1, tm, LANES), F32)]
        args = list(mix_in) + [expand]
        in_specs.append(_resident(expand.shape))
    else:
        args = [mix_in]
        in_specs = [row(mix_in.shape[1])]
    args += [x2d, w_out, w_up, w_down, ln_gain, ln_bias]
    in_specs += [row(d), _resident(w_out.shape[1:], (mixer_layer,)),
                 _resident(w_up.shape[1:], (layer,)), _resident(w_down.shape[1:], (layer,)),
                 _resident(ln_gain.shape[1:], (layer,)), _resident(ln_bias.shape[1:], (layer,))]
    return pl.pallas_call(
        functools.partial(_tail_kernel, merge=merge, alpha=alpha),
        grid=(m // tm,),
        in_specs=in_specs,
        out_specs=row(d),
        out_shape=jax.ShapeDtypeStruct((m, d), F32),
        scratch_shapes=scratch,
        compiler_params=_cparams(("parallel",)),
        name="tail_attn" if merge else "tail_ret",
    )(*args)


def _rope_tables(seq):
    d = RET_QK_DIM
    inv = 1.0 / (ROPE_BASE ** jnp.linspace(0.0, 1.0, d // 2, dtype=jnp.float32))
    ang = jnp.arange(seq, dtype=jnp.float32)[:, None] * inv[None, :]
    return jnp.cos(ang), jnp.sin(ang)


def kernel(x, ret_w_in, ret_w_out, attn_w_in, attn_w_out, rel_bias, ffn_w_up, ffn_w_down, ln_gain, ln_bias):
    batch, seq, d = x.shape
    depth = ffn_w_up.shape[0]
    alpha = (2 * depth) ** 0.25
    cos, sin = _rope_tables(seq)
    bias = _band_bias(rel_bias)
    ret_in, ret_out = _ret_w_in_prep(ret_w_in), _cast_bf16(ret_w_out)
    att_qv, att_kt = _attn_w_in_prep(attn_w_in)
    att_out = _cast_bf16(attn_w_out)
    w_up, w_down = _cast_bf16(ffn_w_up), _cast_bf16(ffn_w_down)
    h = x.reshape(batch * seq, d)
    for i in range(depth):
        j = i // 2
        if i % 2 == 0:
            p = _proj_rope(h, ret_in, j, cos, sin, seq)
            mix_in, w_out = _retention_core(p, batch, seq), ret_out
        else:
            outs = []
            for g, (_, dil) in enumerate(DILATED_PAIRS):
                qv, kt = _proj_attn(h, att_qv, att_kt, j, g, dil, batch, seq)
                outs.append(_attn_group(qv, kt, bias, g))
            mix_in = tuple(a[0] for a in outs) + tuple(a[1] for a in outs) + tuple(a[2] for a in outs)
            w_out = att_out
        h = _tail(mix_in, h, w_out, j, w_up, w_down, ln_gain, ln_bias, i, alpha)
    return h.reshape(batch, seq, d)
```

```python
import functools
import math

import numpy as np
import jax
import jax.numpy as jnp
from jax import lax
from jax.experimental import pallas as pl
from jax.experimental.pallas import tpu as pltpu

F32 = jnp.float32
BF16 = jnp.bfloat16

D_MODEL = 1024
RET_QK_DIM = 256
RET_HEADS = D_MODEL // RET_QK_DIM
RET_V_DIM = 2 * RET_QK_DIM
RET_QK = RET_HEADS * RET_QK_DIM
RET_V = RET_HEADS * RET_V_DIM
ROPE_BASE = 10000.0
DILATED_PAIRS = ((128, 1), (512, 4), (2048, 16))
ATT_GROUPS = 3
ATT_HEAD_DIM = 64
ATT_HEADS = D_MODEL // ATT_HEAD_DIM
ATT_GROUP_W = ATT_HEADS * ATT_HEAD_DIM
BAND_BLOCK = 128
NUM_BUCKETS = 32
MAX_DISTANCE = 2048
D_FF = ((8 * D_MODEL + 3 * 256 - 1) // (3 * 256)) * 256
LN_EPS = 1e-5
GN_EPS = 1e-5

LANES = 128
VMEM_LIMIT_BYTES = 56 * 1024 * 1024
RET_CHUNK = 256
RET_CHUNKS_PER_STEP = 2
PROJ_TM = 1024
PROJ_ROPE_TM = 1024
PROJ_CHUNK = 512
PROJ_KT_TOKENS = 256
ATT_ROWS_MAX = 1024
ATT_UNROLL = 8
TAIL_TM = 512
CAST_BLOCK_BYTES = 12 * 1024 * 1024
FF_CHUNK = 256
MASK_VALUE = float("-inf")
LOG2E = math.log2(math.e)
QUERY_SCALE = ATT_HEAD_DIM ** -0.5 * LOG2E


def _cparams(sem):
    return pltpu.CompilerParams(dimension_semantics=sem, vmem_limit_bytes=VMEM_LIMIT_BYTES)


def _resident(shape, lead=()):
    nd = len(shape)
    return pl.BlockSpec((None,) * len(lead) + tuple(shape), lambda *_: tuple(lead) + (0,) * nd,
                        pipeline_mode=pl.Buffered(1))


def _cast_kernel(w_ref, o_ref):
    o_ref[...] = w_ref[...].astype(BF16)


def _cast_rows(r, c):
    fits = [t for t in range(16, r + 1, 16) if r % t == 0 and t * c * 4 <= CAST_BLOCK_BYTES]
    assert fits
    return fits[-1]


def _cast_bf16(w):
    layers, r, c = w.shape
    tr = _cast_rows(r, c)
    spec = pl.BlockSpec((None, tr, c), lambda l, i: (l, i, 0))
    return pl.pallas_call(
        _cast_kernel, grid=(layers, r // tr), in_specs=[spec], out_specs=spec,
        out_shape=jax.ShapeDtypeStruct(w.shape, BF16),
        compiler_params=_cparams(("parallel", "parallel")), name="cast_bf16",
    )(w)


def _ret_w_in_kernel(w_ref, perm_ref, o_ref):
    wb = w_ref[...].astype(BF16)
    for h in range(2 * RET_HEADS):
        cols = slice(h * RET_QK_DIM, (h + 1) * RET_QK_DIM)
        o_ref[:, cols] = jnp.dot(wb[:, cols], perm_ref[...], preferred_element_type=F32).astype(BF16)
    o_ref[:, 2 * RET_QK:] = wb[:, 2 * RET_QK:]


def _ret_w_in_prep(w):
    layers, r, c = w.shape
    tr = _cast_rows(r, c)
    d = RET_QK_DIM
    perm = np.zeros((d, d), np.float32)
    perm[2 * np.arange(d // 2), np.arange(d // 2)] = 1.0
    perm[2 * np.arange(d // 2) + 1, d // 2 + np.arange(d // 2)] = 1.0
    spec = pl.BlockSpec((None, tr, c), lambda l, i: (l, i, 0))
    return pl.pallas_call(
        _ret_w_in_kernel, grid=(layers, r // tr),
        in_specs=[spec, pl.BlockSpec((d, d), lambda l, i: (0, 0))], out_specs=spec,
        out_shape=jax.ShapeDtypeStruct(w.shape, BF16),
        compiler_params=_cparams(("parallel", "parallel")), name="ret_w_in_prep",
    )(w, jnp.asarray(perm, BF16))


def _attn_w_in_kernel(q_ref, k_ref, v_ref, qv_ref, kt_ref):
    w = ATT_GROUP_W
    qv_ref[:, :w] = q_ref[...].astype(BF16)
    qv_ref[:, w:] = v_ref[...].astype(BF16)
    kt_ref[...] = k_ref[...].T.astype(BF16)


def _attn_w_in_prep(w):
    layers, d, _ = w.shape
    g, gw = ATT_GROUPS, ATT_GROUP_W
    part = lambda p: pl.BlockSpec((None, d, gw), lambda l, i: (l, 0, p * g + i))
    return pl.pallas_call(
        _attn_w_in_kernel, grid=(layers, g),
        in_specs=[part(0), part(1), part(2)],
        out_specs=[pl.BlockSpec((None, None, d, 2 * gw), lambda l, i: (l, i, 0, 0)),
                   pl.BlockSpec((None, None, gw, d), lambda l, i: (l, i, 0, 0))],
        out_shape=[jax.ShapeDtypeStruct((layers, g, d, 2 * gw), BF16),
                   jax.ShapeDtypeStruct((layers, g, gw, d), BF16)],
        compiler_params=_cparams(("parallel", "parallel")), name="attn_w_in_prep",
    )(w, w, w)


def _proj_rope_kernel(x_ref, w_ref, cos_ref, sin_ref, o_ref):
    xb = x_ref[...].astype(BF16)
    n = w_ref.shape[1]
    half = RET_QK_DIM // 2
    for lo in range(2 * RET_QK + RET_V, n, PROJ_CHUNK):
        acc = jnp.dot(xb, w_ref[:, lo:lo + PROJ_CHUNK], preferred_element_type=F32)
        o_ref[:, lo:lo + PROJ_CHUNK] = (acc * (0.5 * jnp.tanh(0.5 * acc) + 0.5)).astype(BF16)
    for part in range(2):
        scale = 1.0 if part == 0 else RET_QK_DIM ** -0.5
        c = cos_ref[...] * scale
        s = sin_ref[...] * scale
        for h in range(RET_HEADS):
            lo = part * RET_QK + h * RET_QK_DIM
            acc = jnp.dot(xb, w_ref[:, lo:lo + RET_QK_DIM], preferred_element_type=F32)
            a, b = acc[:, :half], acc[:, half:]
            o_ref[:, lo:lo + half] = (a * c - b * s).astype(BF16)
            o_ref[:, lo + half:lo + 2 * half] = (a * s + b * c).astype(BF16)
    for lo in range(2 * RET_QK, 2 * RET_QK + RET_V, PROJ_CHUNK):
        o_ref[:, lo:lo + PROJ_CHUNK] = jnp.dot(
            xb, w_ref[:, lo:lo + PROJ_CHUNK], preferred_element_type=F32).astype(BF16)


def _proj_rope(x2d, w, layer, cos, sin, seq):
    m, k = x2d.shape
    n = w.shape[2]
    tm = PROJ_ROPE_TM
    assert m % tm == 0 and seq % tm == 0 and RET_V % PROJ_CHUNK == 0
    nseq = seq // tm
    table = pl.BlockSpec((tm, RET_QK_DIM // 2), lambda i: (i % nseq, 0))
    return pl.pallas_call(
        _proj_rope_kernel,
        grid=(m // tm,),
        in_specs=[pl.BlockSpec((tm, k), lambda i: (i, 0)), _resident(w.shape[1:], (layer,)),
                  table, table],
        out_specs=pl.BlockSpec((tm, n), lambda i: (i, 0)),
        out_shape=jax.ShapeDtypeStruct((m, n), BF16),
        compiler_params=_cparams(("parallel",)),
        name="proj_rope",
    )(x2d, w, cos, sin)


def _proj_attn_kernel(x_ref, w_ref, wkt_ref, qv_ref, kt_ref, slab_ref, xb_ref):
    nres, rows = qv_ref.shape[0], qv_ref.shape[1]
    tm, k = xb_ref.shape
    w = ATT_GROUP_W
    if nres == 1:
        xb_ref[...] = x_ref[...].astype(BF16)
    else:
        for s in range(k // LANES):
            slab_ref[...] = x_ref[..., s * LANES:(s + 1) * LANES].reshape(tm, LANES)
            for r in range(nres):
                xb_ref[r * rows:(r + 1) * rows, s * LANES:(s + 1) * LANES] = (
                    slab_ref[pl.ds(r, rows, stride=nres), :].astype(BF16))

    for lo in range(0, 2 * w, PROJ_CHUNK):
        acc = jnp.dot(xb_ref[...], w_ref[:, lo:lo + PROJ_CHUNK], preferred_element_type=F32)
        if lo < w:
            acc = acc * QUERY_SCALE
        res = acc.astype(BF16)
        for r in range(nres):
            qv_ref[r, :, lo:lo + PROJ_CHUNK] = res[r * rows:(r + 1) * rows, :]

    for t0 in range(0, tm, PROJ_KT_TOKENS):
        acc_t = lax.dot_general(wkt_ref[...], xb_ref[t0:t0 + PROJ_KT_TOKENS, :],
                                (((1,), (1,)), ((), ())), preferred_element_type=F32).astype(BF16)
        for lo in range(0, PROJ_KT_TOKENS, BAND_BLOCK):
            r, c = (t0 + lo) // rows, ((t0 + lo) % rows) // BAND_BLOCK
            kt_ref[r, c] = acc_t[:, lo:lo + BAND_BLOCK]


def _proj_attn(x2d, w_qv, w_kt, layer, g, dil, batch, seq):
    m, k = x2d.shape
    w, blk, tm = ATT_GROUP_W, BAND_BLOCK, PROJ_TM
    length = seq // dil
    nres = min(dil, tm // blk)
    rows = tm // nres
    ngrp, nwin = dil // nres, length // rows
    assert dil % nres == 0 and length % rows == 0 and rows % blk == 0
    assert tm % PROJ_KT_TOKENS == 0 and PROJ_KT_TOKENS % blk == 0 and (2 * w) % PROJ_CHUNK == 0
    where = lambda i: (i // (nwin * ngrp), (i // ngrp) % nwin, i % ngrp)
    if ngrp == 1:
        xv = x2d
        x_spec = pl.BlockSpec((tm, k), lambda i: (i, 0))
    else:
        assert nres % 8 == 0
        xv = x2d.reshape(batch, nwin, rows, ngrp, nres, k)
        x_spec = pl.BlockSpec((None, None, rows, None, nres, k),
                              lambda i: (where(i)[0], where(i)[1], 0, where(i)[2], 0, 0))
    return pl.pallas_call(
        _proj_attn_kernel,
        grid=(m // tm,),
        in_specs=[x_spec, _resident(w_qv.shape[2:], (layer, g)), _resident(w_kt.shape[2:], (layer, g))],
        out_specs=[pl.BlockSpec((None, nres, rows, 2 * w),
                                lambda i: (where(i)[0], where(i)[2], where(i)[1], 0)),
                   pl.BlockSpec((None, nres, rows // blk, w, blk),
                                lambda i: (where(i)[0], where(i)[2], where(i)[1], 0, 0))],
        out_shape=[jax.ShapeDtypeStruct((batch, dil, length, 2 * w), BF16),
                   jax.ShapeDtypeStruct((batch, dil, length // blk, w, blk), BF16)],
        scratch_shapes=[pltpu.VMEM((tm, LANES), F32), pltpu.VMEM((tm, k), BF16)],
        compiler_params=_cparams(("parallel",)),
        name=f"proj_attn_d{dil}",
    )(x2d if ngrp == 1 else xv, w_qv, w_kt)


def _ret_kernel(q_ref, k_ref, v_ref, g_ref, intra_ref, xi_ref, zeta_ref, o_ref, state_ref, *, gamma_c):
    @pl.when(pl.program_id(1) == 0)
    def _():
        state_ref[...] = jnp.zeros_like(state_ref)

    dk, dv, c = RET_QK_DIM, RET_V_DIM, RET_CHUNK
    for r0 in range(0, q_ref.shape[0], c):
        rows = slice(r0, r0 + c)
        for h in range(RET_HEADS):
            q = q_ref[rows, h * dk:(h + 1) * dk]
            k = k_ref[rows, h * dk:(h + 1) * dk]
            v = v_ref[rows, h * dv:(h + 1) * dv]
            s = lax.dot_general(q, k, (((1,), (1,)), ((), ())), preferred_element_type=F32)
            s = (s * intra_ref[h]).astype(BF16)
            st = state_ref[h]
            qx = (q.astype(F32) * xi_ref[h]).astype(BF16)
            y = jnp.dot(jnp.concatenate([s, qx], axis=1),
                        jnp.concatenate([v, st.astype(BF16)], axis=0), preferred_element_type=F32)
            kz = (k.astype(F32) * zeta_ref[h]).astype(BF16)
            upd = lax.dot_general(kz, v, (((0,), (0,)), ((), ())), preferred_element_type=F32)
            state_ref[h] = gamma_c[h] * st + upd
            mu = jnp.mean(y, axis=-1, keepdims=True)
            yc = y - mu
            var = jnp.mean(yc * yc, axis=-1, keepdims=True)
            yn = yc * lax.rsqrt(var + GN_EPS)
            gate = g_ref[rows, h * dv:(h + 1) * dv].astype(F32)
            o_ref[rows, h * dv:(h + 1) * dv] = (gate * yn).astype(BF16)


def _ret_tables(c):
    h = np.arange(RET_HEADS, dtype=np.float64)
    log_gamma = np.log(1.0 - 2.0 ** (-5.0 - h))
    idx = np.arange(c, dtype=np.float64)
    diff = idx[:, None] - idx[None, :]
    intra = np.where(diff[None] >= 0, np.exp(np.maximum(diff, 0.0)[None] * log_gamma[:, None, None]), 0.0)
    xi = np.exp((idx[None, :] + 1.0) * log_gamma[:, None])[..., None]
    zeta = np.exp((c - 1.0 - idx[None, :]) * log_gamma[:, None])[..., None]
    gamma_c = tuple(float(np.float32(v)) for v in np.exp(c * log_gamma))
    return (jnp.asarray(intra, F32), jnp.asarray(xi, F32), jnp.asarray(zeta, F32), gamma_c)


def _retention_core(p, batch, seq):
    c = RET_CHUNK * RET_CHUNKS_PER_STEP
    assert seq % c == 0
    nc = seq // c
    intra, xi, zeta, gamma_c = _ret_tables(RET_CHUNK)
    row = lambda b, i: b * nc + i
    qk = lambda part: pl.BlockSpec((c, RET_QK), lambda b, i: (row(b, i), part))
    wide = lambda part: pl.BlockSpec((c, RET_V), lambda b, i: (row(b, i), part))
    return pl.pallas_call(
        functools.partial(_ret_kernel, gamma_c=gamma_c),
        grid=(batch, nc),
        in_specs=[qk(0), qk(1), wide(1), wide(2),
                  _resident(intra.shape), _resident(xi.shape), _resident(zeta.shape)],
        out_specs=wide(0),
        out_shape=jax.ShapeDtypeStruct((batch * seq, RET_V), BF16),
        scratch_shapes=[pltpu.VMEM((RET_HEADS, RET_QK_DIM, RET_V_DIM), F32)],
        compiler_params=_cparams(("parallel", "arbitrary")),
        name="retention_core",
    )(p, p, p, p, intra, xi, zeta)


def _t5_bucket(dist):
    max_exact = NUM_BUCKETS // 2
    d_f = jnp.maximum(dist, 1).astype(jnp.float32)
    large = max_exact + (jnp.log(d_f / max_exact) / math.log(MAX_DISTANCE / max_exact)
                         * (NUM_BUCKETS - max_exact)).astype(jnp.int32)
    large = jnp.minimum(large, NUM_BUCKETS - 1)
    return jnp.where(dist < max_exact, dist, large)


def _band_buckets():
    blk = BAND_BLOCK
    qi = jnp.arange(blk)[:, None]
    kj = jnp.arange(2 * blk)[None, :]
    delta = qi + blk - kj
    out = []
    for window, dil in DILATED_PAIRS:
        steps = window // dil
        ok = (delta >= 0) & (delta <= steps)
        out.append(jnp.where(ok, _t5_bucket(jnp.maximum(delta, 0) * dil), -1))
    return jnp.stack(out).astype(jnp.int32)


def _bias_kernel(tab_ref, bkt_ref, o_ref):
    g = pl.program_id(0)
    bkt = bkt_ref[0]
    col = lax.broadcasted_iota(jnp.int32, bkt.shape, 1)

    def head(h, carry):
        acc = jnp.full(bkt.shape, MASK_VALUE, F32)
        for b in range(NUM_BUCKETS):
            acc = jnp.where(bkt == b, tab_ref[g * ATT_HEADS + h, b] * LOG2E, acc)
        o_ref[0, h] = acc
        o_ref[1, h] = jnp.where(col < BAND_BLOCK, MASK_VALUE, acc)
        return carry

    lax.fori_loop(0, ATT_HEADS, head, 0)


def _band_bias(rel_bias):
    gh = rel_bias.shape[0]
    blk = BAND_BLOCK
    return pl.pallas_call(
        _bias_kernel,
        grid=(gh // ATT_HEADS,),
        in_specs=[pl.BlockSpec(memory_space=pltpu.SMEM),
                  pl.BlockSpec((1, blk, 2 * blk), lambda i: (i, 0, 0))],
        out_specs=pl.BlockSpec((2, ATT_HEADS, blk, 2 * blk), lambda i: (0, i, 0, 0)),
        out_shape=jax.ShapeDtypeStruct((2, gh, blk, 2 * blk), F32),
        compiler_params=_cparams(("parallel",)),
        name="band_bias",
    )(rel_bias, _band_buckets())


def _stat_lane(h):
    return h // 2 + (ATT_HEAD_DIM if h % 2 == 0 else 0)


def _attn_kernel(q_ref, v_ref, vh_ref, kt_ref, kth_ref, bias_ref, o_ref, m_ref, d_ref, *, nblk):
    blk, hd = BAND_BLOCK, ATT_HEAD_DIM
    first_prog = pl.program_id(2) == 0
    lane = lax.broadcasted_iota(jnp.int32, (blk, LANES), 1)
    vlane = lax.broadcasted_iota(jnp.int32, (2 * blk, LANES), 1)
    zeros_k = jnp.zeros((hd, 2 * blk), BF16)
    ones_v = jnp.ones((2 * blk, LANES), BF16)

    def block(t, carry):
        r, n = lax.div(t, nblk), lax.rem(t, nblk)
        row = pl.multiple_of(n * blk, blk)
        pn = jnp.maximum(n - 1, 0)
        prow = pl.multiple_of(pn * blk, blk)
        first_blk = n == 0
        table = jnp.logical_and(first_blk, first_prog).astype(jnp.int32)
        m_tile = jnp.zeros((blk, LANES), F32)
        d_tile = jnp.ones((blk, LANES), F32)
        for hp in range(ATT_HEADS // 2):
            pair = slice(hp * LANES, (hp + 1) * LANES)
            q = q_ref[r, pl.ds(row, blk), pair]
            vp = jnp.where(first_blk, vh_ref[r, :, pair], v_ref[r, pl.ds(prow, blk), pair])
            vb = jnp.concatenate([vp, v_ref[r, pl.ds(row, blk), pair]], axis=0)
            parts = []
            for hh in range(2):
                h = 2 * hp + hh
                hrows = slice(h * hd, (h + 1) * hd)
                kp = jnp.where(first_blk, kth_ref[r, 0, hrows, :], kt_ref[r, pn, hrows, :])
                kbt = jnp.concatenate([kp, kt_ref[r, n, hrows, :]], axis=1)
                rhs = jnp.concatenate([kbt, zeros_k] if hh == 0 else [zeros_k, kbt], axis=0)
                s = jnp.dot(q, rhs, preferred_element_type=F32) + bias_ref[table, h]
                m = jnp.max(s, axis=-1, keepdims=True)
                e = jnp.exp2(s - m).astype(BF16)
                own = (vlane < hd) if hh == 0 else (vlane >= hd)
                part = jnp.dot(e, jnp.where(own, vb, ones_v), preferred_element_type=F32)
                stat = lane == _stat_lane(h)
                m_tile = jnp.where(stat, m, m_tile)
                d_tile = jnp.where(stat, part, d_tile)
                parts.append(part)
            o_ref[r, pl.ds(row, blk), pair] = jnp.where(lane < hd, parts[0], parts[1]).astype(BF16)
        m_ref[r, pl.ds(row, blk), :] = m_tile
        d_ref[r, pl.ds(row, blk), :] = d_tile
        return carry

    lax.fori_loop(0, q_ref.shape[0] * nblk, block, 0, unroll=ATT_UNROLL)


def _attn_group(qv, kt, bias, g):
    window, dil = DILATED_PAIRS[g]
    assert window // dil == BAND_BLOCK
    blk, w = BAND_BLOCK, ATT_GROUP_W
    batch, _, length, _ = qv.shape
    rows = min(length, ATT_ROWS_MAX)
    nres = min(dil, ATT_ROWS_MAX // rows)
    assert length % rows == 0 and rows % blk == 0 and dil % nres == 0
    nprog, nblk = length // rows, rows // blk
    main = lambda part: pl.BlockSpec((None, nres, rows, w), lambda b, r, s: (b, r, s, part))
    prev_blk = lambda s: jnp.maximum(s * nblk - 1, 0)
    return pl.pallas_call(
        functools.partial(_attn_kernel, nblk=nblk),
        grid=(batch, dil // nres, nprog),
        in_specs=[main(0), main(1),
                  pl.BlockSpec((None, nres, blk, w), lambda b, r, s: (b, r, prev_blk(s), 1)),
                  pl.BlockSpec((None, nres, nblk, w, blk), lambda b, r, s: (b, r, s, 0, 0)),
                  pl.BlockSpec((None, nres, 1, w, blk), lambda b, r, s: (b, r, prev_blk(s), 0, 0)),
                  pl.BlockSpec((2, ATT_HEADS, blk, 2 * blk), lambda b, r, s: (0, g, 0, 0),
                               pipeline_mode=pl.Buffered(1))],
        out_specs=[pl.BlockSpec((None, nres, rows, w), lambda b, r, s: (b, r, s, 0)),
                   pl.BlockSpec((None, nres, rows, LANES), lambda b, r, s: (b, r, s, 0)),
                   pl.BlockSpec((None, nres, rows, LANES), lambda b, r, s: (b, r, s, 0))],
        out_shape=[jax.ShapeDtypeStruct((batch, dil, length, w), BF16),
                   jax.ShapeDtypeStruct((batch, dil, length, LANES), F32),
                   jax.ShapeDtypeStruct((batch, dil, length, LANES), F32)],
        compiler_params=_cparams(("parallel", "parallel", "arbitrary")),
        name=f"attn_g{g}",
    )(qv, qv, qv, kt, kt, bias)


def _layer_norm(v, gain, bias):
    mu = jnp.mean(v, axis=-1, keepdims=True)
    vc = v - mu
    var = jnp.mean(vc * vc, axis=-1, keepdims=True)
    return vc * lax.rsqrt(var + LN_EPS) * gain + bias


def _token_order(src_ref, dst_ref):
    dil, rows, width = src_ref.shape
    for r in range(dil):
        for s in range(width // LANES):
            dst_ref[s, pl.ds(r, rows, stride=dil), :] = (
                src_ref[r, :, s * LANES:(s + 1) * LANES].astype(F32))


def _merge_groups(o_refs, m_refs, d_refs, ex_ref, scratch):
    nums, maxs, dens = [], [], []
    for o_ref, m_ref, d_ref in zip(o_refs, m_refs, d_refs):
        if o_ref.shape[0] == 1:
            nums.append(o_ref[0].astype(F32))
            maxs.append(m_ref[0])
            dens.append(d_ref[0])
        else:
            so_ref, sm_ref, sd_ref = (scratch.pop(0) for _ in range(3))
            _token_order(o_ref, so_ref)
            _token_order(m_ref, sm_ref)
            _token_order(d_ref, sd_ref)
            nums.append(jnp.concatenate([so_ref[s] for s in range(so_ref.shape[0])], axis=1))
            maxs.append(sm_ref[0])
            dens.append(sd_ref[0])
    mx = jnp.maximum(jnp.maximum(maxs[0], maxs[1]), maxs[2])
    es = [jnp.exp2(m - mx) for m in maxs]
    inv = 1.0 / (es[0] * dens[0] + es[1] * dens[1] + es[2] * dens[2])
    z = None
    for e, num in zip(es, nums):
        wgt = (e * inv).astype(BF16)
        term = jnp.dot(wgt, ex_ref[...], preferred_element_type=F32) * num
        z = term if z is None else z + term
    return z.astype(BF16)


def _tail_kernel(*refs, merge, alpha):
    if merge:
        g = ATT_GROUPS
        o_refs, m_refs, d_refs = refs[:g], refs[g:2 * g], refs[2 * g:3 * g]
        ex_ref, x_ref, wo_ref, wu_ref, wd_ref, gain_ref, bias_ref, out_ref, *scratch = refs[3 * g:]
    else:
        z_ref, x_ref, wo_ref, wu_ref, wd_ref, gain_ref, bias_ref, out_ref = refs
    z = _merge_groups(o_refs, m_refs, d_refs, ex_ref, scratch) if merge else z_ref[...]
    mix = jnp.dot(z, wo_ref[...], preferred_element_type=F32)
    x1 = _layer_norm(alpha * x_ref[...] + mix, gain_ref[0:1, :], bias_ref[0:1, :])
    x1b = x1.astype(BF16)
    acc = alpha * x1
    for c0 in range(0, D_FF, FF_CHUNK):
        gate = jnp.dot(x1b, wu_ref[:, c0:c0 + FF_CHUNK], preferred_element_type=F32)
        up = jnp.dot(x1b, wu_ref[:, D_FF + c0:D_FF + c0 + FF_CHUNK], preferred_element_type=F32)
        hid = gate * (1.0 / (1.0 + jnp.exp(-gate))) * up
        acc = acc + jnp.dot(hid.astype(BF16), wd_ref[c0:c0 + FF_CHUNK, :], preferred_element_type=F32)
    out_ref[...] = _layer_norm(acc, gain_ref[1:2, :], bias_ref[1:2, :])


def _tail(mix_in, x2d, w_out, mixer_layer, w_up, w_down, ln_gain, ln_bias, layer, alpha):
    m, d = x2d.shape
    tm = TAIL_TM
    assert m % tm == 0 and D_FF % FF_CHUNK == 0
    merge = isinstance(mix_in, tuple)
    row = lambda width: pl.BlockSpec((tm, width), lambda i: (i, 0))
    scratch = []
    if merge:
        expand = np.zeros((LANES, ATT_GROUP_W), np.float32)
        for h in range(ATT_HEADS):
            expand[_stat_lane(h), h * ATT_HEAD_DIM:(h + 1) * ATT_HEAD_DIM] = 1.0
        expand = jnp.asarray(expand, BF16)
        in_specs = []
        for a in mix_in:
            batch, dil, length, width = a.shape
            nseq = dil * length // tm
            assert tm % (dil * 16) == 0
            in_specs.append(pl.BlockSpec((None, dil, tm // dil, width),
                                         lambda i, nseq=nseq: (i // nseq, 0, i % nseq, 0)))
        for a in mix_in[:ATT_GROUPS]:
            if a.shape[1] > 1:
                scratch += [pltpu.VMEM((ATT_GROUP_W // LANES, tm, LANES), F32),
                            pltpu.VMEM((1, tm, LANES), F32), pltpu.VMEM((1, tm, LANES), F32)]
        args = list(mix_in) + [expand]
        in_specs.append(_resident(expand.shape))
    else:
        args = [mix_in]
        in_specs = [row(mix_in.shape[1])]
    args += [x2d, w_out, w_up, w_down, ln_gain, ln_bias]
    in_specs += [row(d), _resident(w_out.shape[1:], (mixer_layer,)),
                 _resident(w_up.shape[1:], (layer,)), _resident(w_down.shape[1:], (layer,)),
                 _resident(ln_gain.shape[1:], (layer,)), _resident(ln_bias.shape[1:], (layer,))]
    return pl.pallas_call(
        functools.partial(_tail_kernel, merge=merge, alpha=alpha),
        grid=(m // tm,),
        in_specs=in_specs,
        out_specs=row(d),
        out_shape=jax.ShapeDtypeStruct((m, d), F32),
        scratch_shapes=scratch,
        compiler_params=_cparams(("parallel",)),
        name="tail_attn" if merge else "tail_ret",
    )(*args)


def _rope_tables(seq):
    d = RET_QK_DIM
    inv = 1.0 / (ROPE_BASE ** jnp.linspace(0.0, 1.0, d // 2, dtype=jnp.float32))
    ang = jnp.arange(seq, dtype=jnp.float32)[:, None] * inv[None, :]
    return jnp.cos(ang), jnp.sin(ang)


def kernel(x, ret_w_in, ret_w_out, attn_w_in, attn_w_out, rel_bias, ffn_w_up, ffn_w_down, ln_gain, ln_bias):
    batch, seq, d = x.shape
    depth = ffn_w_up.shape[0]
    alpha = (2 * depth) ** 0.25
    cos, sin = _rope_tables(seq)
    bias = _band_bias(rel_bias)
    ret_in, ret_out = _ret_w_in_prep(ret_w_in), _cast_bf16(ret_w_out)
    att_qv, att_kt = _attn_w_in_prep(attn_w_in)
    att_out = _cast_bf16(attn_w_out)
    w_up, w_down = _cast_bf16(ffn_w_up), _cast_bf16(ffn_w_down)
    h = x.reshape(batch * seq, d)
    for i in range(depth):
        j = i // 2
        if i % 2 == 0:
            p = _proj_rope(h, ret_in, j, cos, sin, seq)
            mix_in, w_out = _retention_core(p, batch, seq), ret_out
        else:
            outs = []
            for g, (_, dil) in enumerate(DILATED_PAIRS):
                qv, kt = _proj_attn(h, att_qv, att_kt, j, g, dil, batch, seq)
                outs.append(_attn_group(qv, kt, bias, g))
            mix_in = tuple(a[0] for a in outs) + tuple(a[1] for a in outs) + tuple(a[2] for a in outs)
            w_out = att_out
        h = _tail(mix_in, h, w_out, j, w_up, w_down, ln_gain, ln_bias, i, alpha)
    return h.reshape(batch, seq, d)
```
